```python
import math
import jax, jax.numpy as jnp
from jax import lax
import numpy as np

D_MODEL = 2048
BATCH = 32
SEQ = 256
DEPTH = 4
DEC_BATCH = 8
DEC_SEQ = 2048
PAST_LEN = 512

GRID_W = 64
CHUNK = 128
A_GROUPS = 8
A_GROUP_DIM = 128
A_WIDTH = A_GROUPS * A_GROUP_DIM
NA_HEADS = 8
NA_DH = 128
NA_WIDTH = NA_HEADS * NA_DH
NA_WIN_H = 8
NA_WIN_W = 16
GLA_HEADS = 4
GLA_DK = 128
GLA_DV = 256
GLA_KDIM = GLA_HEADS * GLA_DK
GLA_VDIM = GLA_HEADS * GLA_DV
GLA_RANK = 16
GLA_GATE_TEMP = 16.0
GLA_CHUNK = 64
ROPE_BASE = 10000.0
PEER_HEADS = 8
PEER_NKEYS = 128
PEER_QDIM = 256
PEER_TOPK = 16
N_EXPERTS = PEER_NKEYS * PEER_NKEYS
PEER_BLOCK = 128
N_MOD = 6
IN_SIZES = (2 * A_WIDTH, NA_WIDTH, NA_WIDTH, NA_WIDTH, GLA_KDIM, GLA_KDIM, GLA_VDIM, GLA_VDIM,
            GLA_RANK, GLA_RANK, D_MODEL, D_MODEL, D_MODEL)
D_IN = 2 * A_WIDTH + 3 * NA_WIDTH + 2 * GLA_KDIM + 2 * GLA_VDIM + 2 * GLA_RANK + 3 * D_MODEL
ALPHA = (2 * DEPTH) ** 0.25
BETA = (8 * DEPTH) ** -0.25
EPS = 1e-5

kernel_name = 'hybrid_diffusion_gmlp_natten_gla_peer_step'

F32 = jnp.float32


def _layernorm(x, g, b):
    xf = x.astype(F32)
    mu = xf.mean(-1, keepdims=True)
    var = jnp.square(xf - mu).mean(-1, keepdims=True)
    return ((xf - mu) * lax.rsqrt(var + EPS)).astype(x.dtype) * g + b


def _modulate(x, shift, scale):
    return x * (1 + scale) + shift


def _axial_rope(x):
    T, d = x.shape[1], x.shape[-1]
    half = d // 2
    nf = half // 2
    t = jnp.arange(T)
    inv = ROPE_BASE ** (-jnp.arange(nf, dtype=F32) / nf)

    def rot(xp, pos):
        ang = pos.astype(F32)[:, None] * inv
        cos = jnp.cos(ang)[None, :, None, :]
        sin = jnp.sin(ang)[None, :, None, :]
        x1, x2 = xp[..., :nf].astype(F32), xp[..., nf:].astype(F32)
        return jnp.concatenate([x1 * cos - x2 * sin, x1 * sin + x2 * cos], axis=-1)

    out = jnp.concatenate([rot(x[..., :half], t // GRID_W), rot(x[..., half:], t % GRID_W)], axis=-1)
    return out.astype(x.dtype)


def _project(h, w_in):
    z = h @ w_in
    cuts = [int(i) for i in np.cumsum(IN_SIZES)[:-1]]
    return jnp.split(z, cuts, axis=-1)


def _chunk_mlp(a, ln_g, ln_b, ws, bs):
    B_, T, _ = a.shape
    a = jax.nn.gelu(a)
    u, v = jnp.split(a, 2, axis=-1)
    v = _layernorm(v, ln_g, ln_b)
    v = v.reshape(B_, T // CHUNK, CHUNK, A_GROUPS, A_GROUP_DIM)
    v = jnp.einsum('gts,bnsgc->bntgc', ws, v) + bs.T[:, :, None]
    return u * v.reshape(B_, T, A_WIDTH)


def _ctx_attention(q, k, v):
    B_, H, S, dh = q.shape
    scale = dh ** -0.5
    qb = q.reshape(B_, H, S // CHUNK, CHUNK, dh).transpose(2, 0, 1, 3, 4)

    def block(qq):
        s = jnp.einsum('bhqd,bhkd->bhqk', qq, k).astype(F32) * scale
        p = jax.nn.softmax(s, axis=-1).astype(v.dtype)
        return jnp.einsum('bhqk,bhkd->bhqd', p, v)

    o = lax.map(block, qb)
    return o.transpose(1, 2, 0, 3, 4).reshape(B_, H, S, dh)


def _neighbourhood_attention(q, k, v, ck, cv, rpb):
    B_, H, T, dh = q.shape
    R = T // GRID_W
    KH = min(NA_WIN_H, R)
    KW = NA_WIN_W
    scale = dh ** -0.5
    qg = q.reshape(B_, H, R, GRID_W, dh)
    kg = k.reshape(B_, H, R, GRID_W, dh)
    vg = v.reshape(B_, H, R, GRID_W, dh)
    rows = jnp.arange(R)
    cols = jnp.arange(GRID_W)
    rs = jnp.clip(rows - KH // 2, 0, R - KH)
    ridx = rs[:, None] + jnp.arange(KH)[None, :]
    kr = kg[:, :, ridx]
    vr = vg[:, :, ridx]
    cs = jnp.clip(cols - KW // 2, 0, GRID_W - KW)
    col_ok = (cols[None, :] >= cs[:, None]) & (cols[None, :] < cs[:, None] + KW)
    dr = ridx - rows[:, None] + (NA_WIN_H - 1)
    dc = jnp.clip(cols[None, :] - cols[:, None], -(KW - 1), KW - 1) + (KW - 1)
    bias = rpb[:, dr[:, None, :, None], dc[None, :, None, :]]
    s_win = jnp.einsum('bhrqd,bhrjkd->bhrqjk', qg, kr).astype(F32) * scale + bias.astype(F32)
    s_win = jnp.where(col_ok[:, None, :], s_win, -jnp.inf)
    s_ctx = jnp.einsum('bhrqd,bhld->bhrql', qg, ck).astype(F32) * scale
    n_win = KH * GRID_W
    s = jnp.concatenate([s_win.reshape(B_, H, R, GRID_W, n_win), s_ctx], axis=-1)
    p = jax.nn.softmax(s, axis=-1).astype(v.dtype)
    p_win = p[..., :n_win].reshape(B_, H, R, GRID_W, KH, GRID_W)
    o = (jnp.einsum('bhrqjk,bhrjkd->bhrqd', p_win, vr)
         + jnp.einsum('bhrql,bhld->bhrqd', p[..., n_win:], cv))
    return o.reshape(B_, H, T, dh)


def _gla_scan(q, k, v, g, s0):
    B_, H, T, _ = q.shape
    dv = v.shape[-1]
    n = T // GLA_CHUNK

    def chunks(t):
        return t.astype(F32).reshape(B_, H, n, GLA_CHUNK, t.shape[-1]).transpose(2, 0, 1, 3, 4)

    qc, kc, vc = chunks(q), chunks(k), chunks(v)
    bc = jnp.cumsum(chunks(g), axis=3)
    lower = jnp.tril(jnp.ones((GLA_CHUNK, GLA_CHUNK), bool))[:, :, None]

    def step(state, inp):
        qi, ki, vi, bi = inp
        total = bi[:, :, -1:, :]
        o_inter = jnp.einsum('bhtd,bhde->bhte', qi * jnp.exp(bi), state)
        decay = jnp.exp(jnp.where(lower, bi[:, :, :, None, :] - bi[:, :, None, :, :], -jnp.inf))
        scores = jnp.einsum('bhtd,bhsd,bhtsd->bhts', qi, ki, decay)
        o_intra = jnp.einsum('bhts,bhse->bhte', scores, vi)
        state = (jnp.exp(total[:, :, 0, :, None]) * state
                 + jnp.einsum('bhsd,bhse->bhde', ki * jnp.exp(total - bi), vi))
        return state, o_inter + o_intra

    s_fin, o = lax.scan(step, s0.astype(F32), (qc, kc, vc, bc))
    return o.transpose(1, 2, 0, 3, 4).reshape(B_, H, T, dv), s_fin


def _gla_branch(q_c, k_c, v_c, r_c, lr_f, lr_b, s0, lp, latent):
    B_, T, _ = q_c.shape
    q = q_c.reshape(B_, T, GLA_HEADS, GLA_DK)
    k = k_c.reshape(B_, T, GLA_HEADS, GLA_DK)
    v = v_c.reshape(B_, T, GLA_HEADS, GLA_DV)
    if latent:
        q, k = _axial_rope(q), _axial_rope(k)
    log_f = jax.nn.log_sigmoid((lr_f @ lp['gla_wg_f'] + lp['gla_bg_f']).astype(F32)) / GLA_GATE_TEMP
    log_b = jax.nn.log_sigmoid((lr_b @ lp['gla_wg_b'] + lp['gla_bg_b']).astype(F32)) / GLA_GATE_TEMP

    def heads(t):
        return t.reshape(B_, T, GLA_HEADS, -1).transpose(0, 2, 1, 3)

    q = heads(q) * GLA_DK ** -0.5
    k, v, lf, lb = heads(k), heads(v), heads(log_f), heads(log_b)
    o_f, s_f = _gla_scan(q, k, v, lf, s0[:, 0])

    def flip(t):
        return jnp.flip(t, axis=2)

    o_b, s_b = _gla_scan(flip(q), flip(k), flip(v), flip(lb), s0[:, 1])
    o = (o_f + flip(o_b)).transpose(0, 2, 1, 3)
    o = o * lax.rsqrt(jnp.square(o).mean(-1, keepdims=True) + EPS)
    o = o.reshape(B_, T, GLA_VDIM).astype(v_c.dtype) * lp['gla_norm_g']
    y = o * jax.nn.silu(r_c)
    return y, jnp.stack([s_f, s_b], axis=1).astype(v_c.dtype)


def _merge(y_a, y_b, y_c, gate_a, gate_b, gate_c, lp):
    m = (jax.nn.sigmoid(gate_a) * (y_a @ lp['w_br_a'])
         + jax.nn.sigmoid(gate_b) * (y_b @ lp['w_br_b'])
         + jax.nn.sigmoid(gate_c) * (y_c @ lp['w_br_c']))
    return m @ lp['w_out']


def _mixer(h, lp, ctx_k, ctx_v, s0, latent):
    B_, T, _ = h.shape
    (a, q_b, k_b, v_b, q_c, k_c, v_c, r_c, lr_f, lr_b,
     gate_a, gate_b, gate_c) = _project(h, lp['w_in'])
    y_a = _chunk_mlp(a, lp['a_ln_g'], lp['a_ln_b'], lp['a_ws'], lp['a_bs'])

    def heads(t):
        return t.reshape(B_, T, NA_HEADS, NA_DH).transpose(0, 2, 1, 3)

    q_b, k_b, v_b = heads(q_b), heads(k_b), heads(v_b)
    if latent:
        o_b = _neighbourhood_attention(q_b, k_b, v_b, ctx_k, ctx_v, lp['na_rpb'])
    else:
        o_b = _ctx_attention(q_b, k_b, v_b)
        ctx_k, ctx_v = k_b, v_b
        s0 = jnp.zeros((B_, 2, GLA_HEADS, GLA_DK, GLA_DV), h.dtype)
    y_b = o_b.transpose(0, 2, 1, 3).reshape(B_, T, NA_WIDTH)
    y_c, s_fin = _gla_branch(q_c, k_c, v_c, r_c, lr_f, lr_b, s0, lp, latent)
    y = _merge(y_a, y_b, y_c, gate_a, gate_b, gate_c, lp)
    return y, ctx_k, ctx_v, s_fin


def _peer(h, lp):
    B_, T, D = h.shape
    xb = h.reshape(-1, PEER_BLOCK, D)
    wq, k1, k2, U, V = lp['peer_wq'], lp['peer_k1'], lp['peer_k2'], lp['peer_u'], lp['peer_v']

    def block(xt):
        q = (xt @ wq).reshape(-1, PEER_HEADS, PEER_QDIM).astype(F32)
        q1, q2 = jnp.split(q, 2, axis=-1)
        s1 = jnp.einsum('nhd,kd->nhk', q1, k1.astype(F32))
        s2 = jnp.einsum('nhd,kd->nhk', q2, k2.astype(F32))
        t1, i1 = lax.top_k(s1, PEER_TOPK)
        t2, i2 = lax.top_k(s2, PEER_TOPK)
        cand = (t1[..., :, None] + t2[..., None, :]).reshape(-1, PEER_HEADS, PEER_TOPK * PEER_TOPK)
        cidx = (i1[..., :, None] * PEER_NKEYS + i2[..., None, :]).reshape(-1, PEER_HEADS, PEER_TOPK * PEER_TOPK)
        top_s, top_i = lax.top_k(cand, PEER_TOPK)
        idx = jnp.take_along_axis(cidx, top_i, axis=-1)
        g = jax.nn.softmax(top_s, axis=-1).astype(xt.dtype)
        act = jax.nn.gelu(jnp.einsum('nd,nhkd->nhk', xt, U[idx]))
        return jnp.einsum('nhk,nhkd->nd', g * act, V[idx])

    return lax.map(block, xb).reshape(B_, T, D)


def _layer(x, mod, lp, ctx_k, ctx_v, s0, latent):
    sh1, sc1, g1, sh2, sc2, g2 = jnp.split(mod[:, None, :], N_MOD, axis=-1)
    y, k_ctx, v_ctx, s_ctx = _mixer(_modulate(x, sh1, sc1), lp, ctx_k, ctx_v, s0, latent)
    x = _layernorm(ALPHA * x + g1 * y, lp['ln1_g'], lp['ln1_b'])
    x = _layernorm(ALPHA * x + g2 * _peer(_modulate(x, sh2, sc2), lp), lp['ln2_g'], lp['ln2_b'])
    return x, k_ctx, v_ctx, s_ctx


def setup_inputs(seed: int = 0) -> dict:
    key = jax.random.key(seed)
    ks = jax.random.split(key, 40)
    L, D = DEPTH, D_MODEL

    def nrm(k, shape, s):
        return jax.random.normal(k, shape, F32) * s

    return {
        'x_prompt': nrm(ks[0], (BATCH, SEQ, D), 1.0),
        'x_sample': nrm(ks[1], (DEC_BATCH, DEC_SEQ, D), 1.0),
        'cache_k': nrm(ks[2], (DEC_BATCH, L, NA_HEADS, PAST_LEN, NA_DH), 1.0),
        'cache_v': nrm(ks[3], (DEC_BATCH, L, NA_HEADS, PAST_LEN, NA_DH), 1.0),
        'state_gla': nrm(ks[4], (DEC_BATCH, L, 2, GLA_HEADS, GLA_DK, GLA_DV), 0.1),
        'c': nrm(ks[5], (DEC_BATCH, D), 1.0),
        'c_ctx': nrm(ks[6], (D,), 1.0),
        'w_ada': nrm(ks[7], (L, D, N_MOD * D), 0.5 * D ** -0.5),
        'b_ada': nrm(ks[8], (L, N_MOD * D), 0.01),
        'w_in': nrm(ks[9], (L, D, D_IN), D ** -0.5),
        'a_ln_g': 1.0 + nrm(ks[10], (L, A_WIDTH), 0.01),
        'a_ln_b': nrm(ks[11], (L, A_WIDTH), 0.01),
        'a_ws': nrm(ks[12], (L, A_GROUPS, CHUNK, CHUNK), CHUNK ** -0.5),
        'a_bs': 1.0 + nrm(ks[13], (L, A_GROUPS, CHUNK), 0.01),
        'na_rpb': nrm(ks[14], (L, NA_HEADS, 2 * NA_WIN_H - 1, 2 * NA_WIN_W - 1), 0.1),
        'gla_wg_f': nrm(ks[15], (L, GLA_RANK, GLA_KDIM), GLA_RANK ** -0.5),
        'gla_bg_f': nrm(ks[16], (L, GLA_KDIM), 0.01),
        'gla_wg_b': nrm(ks[17], (L, GLA_RANK, GLA_KDIM), GLA_RANK ** -0.5),
        'gla_bg_b': nrm(ks[18], (L, GLA_KDIM), 0.01),
        'gla_norm_g': 1.0 + nrm(ks[19], (L, GLA_VDIM), 0.01),
        'w_br_a': nrm(ks[20], (L, A_WIDTH, D), BETA * A_WIDTH ** -0.5),
        'w_br_b': nrm(ks[21], (L, NA_WIDTH, D), BETA * NA_WIDTH ** -0.5),
        'w_br_c': nrm(ks[22], (L, GLA_VDIM, D), BETA * GLA_VDIM ** -0.5),
        'w_out': nrm(ks[23], (L, D, D), BETA * D ** -0.5),
        'ln1_g': 1.0 + nrm(ks[24], (L, D), 0.01),
        'ln1_b': nrm(ks[25], (L, D), 0.01),
        'ln2_g': 1.0 + nrm(ks[26], (L, D), 0.01),
        'ln2_b': nrm(ks[27], (L, D), 0.01),
        'peer_wq': nrm(ks[28], (L, D, PEER_HEADS * PEER_QDIM), D ** -0.5),
        'peer_k1': nrm(ks[29], (L, PEER_NKEYS, PEER_QDIM // 2), (PEER_QDIM // 2) ** -0.5),
        'peer_k2': nrm(ks[30], (L, PEER_NKEYS, PEER_QDIM // 2), (PEER_QDIM // 2) ** -0.5),
        'peer_u': nrm(ks[31], (L, N_EXPERTS, D), D ** -0.5),
        'peer_v': nrm(ks[32], (L, N_EXPERTS, D), BETA),
    }


def reference(x_prompt, x_sample, cache_k, cache_v, state_gla, c, c_ctx, w_ada, b_ada, w_in,
              a_ln_g, a_ln_b, a_ws, a_bs, na_rpb, gla_wg_f, gla_bg_f, gla_wg_b, gla_bg_b,
              gla_norm_g, w_br_a, w_br_b, w_br_c, w_out, ln1_g, ln1_b, ln2_g, ln2_b,
              peer_wq, peer_k1, peer_k2, peer_u, peer_v):
    xp, xs = x_prompt, x_sample
    new_k, new_v, new_s = [], [], []
    for l in range(DEPTH):
        lp = {
            'w_in': w_in[l], 'a_ln_g': a_ln_g[l], 'a_ln_b': a_ln_b[l], 'a_ws': a_ws[l], 'a_bs': a_bs[l],
            'na_rpb': na_rpb[l], 'gla_wg_f': gla_wg_f[l], 'gla_bg_f': gla_bg_f[l],
            'gla_wg_b': gla_wg_b[l], 'gla_bg_b': gla_bg_b[l], 'gla_norm_g': gla_norm_g[l],
            'w_br_a': w_br_a[l], 'w_br_b': w_br_b[l], 'w_br_c': w_br_c[l], 'w_out': w_out[l],
            'ln1_g': ln1_g[l], 'ln1_b': ln1_b[l], 'ln2_g': ln2_g[l], 'ln2_b': ln2_b[l],
            'peer_wq': peer_wq[l], 'peer_k1': peer_k1[l], 'peer_k2': peer_k2[l],
            'peer_u': peer_u[l], 'peer_v': peer_v[l],
        }
        mod_ctx = jax.nn.silu(c_ctx)[None, :] @ w_ada[l] + b_ada[l]
        mod_lat = jax.nn.silu(c) @ w_ada[l] + b_ada[l]
        xp, k_l, v_l, s_l = _layer(xp, mod_ctx, lp, None, None, None, False)
        new_k.append(k_l)
        new_v.append(v_l)
        new_s.append(s_l)
        xs, _, _, _ = _layer(xs, mod_lat, lp, cache_k[:, l], cache_v[:, l], state_gla[:, l], True)
    return (xp, xs, jnp.stack(new_k, axis=1), jnp.stack(new_v, axis=1), jnp.stack(new_s, axis=1))
```

```python
import functools

import numpy as np
import jax
import jax.numpy as jnp
from jax import lax
from jax.experimental import pallas as pl
from jax.experimental.pallas import tpu as pltpu

F32 = jnp.float32
BF16 = jnp.bfloat16

D_MODEL = 2048
SEQ = 256
DEC_SEQ = 2048
GRID_W = 64
CHUNK = 128
A_GROUPS = 8
A_WIDTH = 1024
NA_HEADS = 8
NA_DH = 128
NA_WIN_H = 8
NA_WIN_W = 16
GLA_HEADS = 4
GLA_DK = 128
GLA_DV = 256
GLA_RANK = 16
GLA_GATE_TEMP = 16.0
GLA_CHUNK = 64
ROPE_BASE = 10000.0
PEER_HEADS = 8
PEER_NKEYS = 128
PEER_TOPK = 16
N_EXPERTS = PEER_NKEYS * PEER_NKEYS
N_MOD = 6
ALPHA = 8.0 ** 0.25
EPS = 1e-5
NEG = -1e30

Z_WIDTH = 14336
COL_A = 0
COL_QB, COL_KB, COL_VB = 2048, 3072, 4096
COL_QC, COL_KC, COL_VC, COL_RC = 5120, 5632, 6144, 7168
COL_GA, COL_GB, COL_GC = 8192, 10240, 12288
LR_START, LR_END = 8192, 8224

VMEM_CAP = 56 * 1024 * 1024


def _cparams(n_axes, vmem_bytes):
    return pltpu.CompilerParams(
        dimension_semantics=("arbitrary",) * n_axes,
        vmem_limit_bytes=min(int(vmem_bytes), VMEM_CAP),
    )


def _mod_row(i, tm, n_ctx_rows):
    start = i * tm
    return jnp.where(start < n_ctx_rows, 0, 1 + (start - n_ctx_rows) // DEC_SEQ)


def _dot(a, b):
    return jnp.dot(a, b, preferred_element_type=F32)


def _dot_nt(a, b):
    return lax.dot_general(a, b, (((1,), (1,)), ((), ())), preferred_element_type=F32)


def _dot_tn(a, b):
    return lax.dot_general(a, b, (((0,), (0,)), ((), ())), preferred_element_type=F32)


def _split_bf16(x):
    hi = x.astype(BF16)
    lo = (x - hi.astype(F32)).astype(BF16)
    return hi, lo


def _ada_kernel(c_ref, w_ref, b_ref, o_ref):
    c = c_ref[...]
    a = (c * jax.nn.sigmoid(c)).astype(BF16)
    o_ref[...] = _dot(a, w_ref[...].astype(BF16)) + b_ref[...]


def _ada(cc, w_ada, b_ada):
    n_layers, d, n = w_ada.shape
    rows = cc.shape[0]
    tn = 1024
    return pl.pallas_call(
        _ada_kernel,
        grid=(n_layers, n // tn),
        in_specs=[
            pl.BlockSpec((rows, d), lambda l, j: (0, 0)),
            pl.BlockSpec((None, d, tn), lambda l, j: (l, 0, j)),
            pl.BlockSpec((None, 1, tn), lambda l, j: (l, 0, j)),
        ],
        out_specs=pl.BlockSpec((None, rows, tn), lambda l, j: (l, 0, j)),
        out_shape=jax.ShapeDtypeStruct((n_layers, rows, n), F32),
        compiler_params=_cparams(2, 2 * d * tn * 4 + 12 * 2**20),
        name="ada_mod",
    )(cc, w_ada, b_ada.reshape(n_layers, 1, n))


def _mm_kernel(a_ref, b_ref, o_ref):
    o_ref[...] = _dot(a_ref[...].astype(BF16), b_ref[...].astype(BF16)).astype(o_ref.dtype)


def _matmul(a, b, out_dtype, tm, tn, name, b_resident=False):
    m, k = a.shape
    _, n = b.shape
    assert m % tm == 0 and n % tn == 0, (m, n, tm, tn)
    est = 2 * (tm * k * a.dtype.itemsize + k * tn * b.dtype.itemsize + tm * tn * 4) + tm * tn * 8
    if b_resident:
        grid = (n // tn, m // tm)
        a_map, b_map, o_map = (lambda j, i: (i, 0)), (lambda j, i: (0, j)), (lambda j, i: (i, j))
    else:
        grid = (m // tm, n // tn)
        a_map, b_map, o_map = (lambda i, j: (i, 0)), (lambda i, j: (0, j)), (lambda i, j: (i, j))
    return pl.pallas_call(
        _mm_kernel,
        grid=grid,
        in_specs=[pl.BlockSpec((tm, k), a_map), pl.BlockSpec((k, tn), b_map)],
        out_specs=pl.BlockSpec((tm, tn), o_map),
        out_shape=jax.ShapeDtypeStruct((m, n), out_dtype),
        compiler_params=_cparams(2, est + 8 * 2**20),
        name=name,
    )(a, b)


def _modulate_kernel(x_ref, sh_ref, sc_ref, h_ref):
    h_ref[...] = (x_ref[...] * (1.0 + sc_ref[...]) + sh_ref[...]).astype(BF16)


def _modulate(x, sh, sc, n_ctx_rows, tm=256):
    m, d = x.shape
    vec = pl.BlockSpec((None, 1, d), lambda i: (_mod_row(i, tm, n_ctx_rows), 0, 0))
    return pl.pallas_call(
        _modulate_kernel,
        grid=(m // tm,),
        in_specs=[pl.BlockSpec((tm, d), lambda i: (i, 0)), vec, vec],
        out_specs=pl.BlockSpec((tm, d), lambda i: (i, 0)),
        out_shape=jax.ShapeDtypeStruct((m, d), BF16),
        compiler_params=_cparams(1, 32 * 2**20),
        name="modulate",
    )(x, sh, sc)


def _res_ln_kernel(x_ref, y_ref, g_ref, lg_ref, lb_ref, sh_ref, sc_ref, xo_ref, h_ref):
    t = ALPHA * x_ref[...] + g_ref[...] * y_ref[...].astype(F32)
    mu = jnp.mean(t, axis=-1, keepdims=True)
    tc = t - mu
    var = jnp.mean(tc * tc, axis=-1, keepdims=True)
    xn = tc * lax.rsqrt(var + EPS) * lg_ref[...] + lb_ref[...]
    xo_ref[...] = xn
    h_ref[...] = (xn * (1.0 + sc_ref[...]) + sh_ref[...]).astype(BF16)


def _res_ln(x, y, g, ln_g, ln_b, sh, sc, n_ctx_rows, name, tm=256):
    m, d = x.shape
    vec = pl.BlockSpec((None, 1, d), lambda i: (_mod_row(i, tm, n_ctx_rows), 0, 0))
    par = pl.BlockSpec((1, d), lambda i: (0, 0))
    row = pl.BlockSpec((tm, d), lambda i: (i, 0))
    return pl.pallas_call(
        _res_ln_kernel,
        grid=(m // tm,),
        in_specs=[row, row, vec, par, par, vec, vec],
        out_specs=[row, row],
        out_shape=[jax.ShapeDtypeStruct((m, d), F32), jax.ShapeDtypeStruct((m, d), BF16)],
        compiler_params=_cparams(1, 40 * 2**20),
        name=name,
    )(x, y, g, ln_g.reshape(1, d), ln_b.reshape(1, d), sh, sc)


def _mixer_a_kernel(a_ref, lg_ref, lb_ref, ws_ref, bs_ref, y_ref, *, n_chunks):
    gw = A_WIDTH // A_GROUPS
    for c in range(n_chunks):
        rows = slice(c * CHUNK, (c + 1) * CHUNK)
        g = jax.nn.gelu(a_ref[rows, :].astype(F32))
        u = g[:, :A_WIDTH]
        v = g[:, A_WIDTH:]
        mu = jnp.mean(v, axis=-1, keepdims=True)
        vc = v - mu
        var = jnp.mean(vc * vc, axis=-1, keepdims=True)
        v = vc * lax.rsqrt(var + EPS) * lg_ref[...] + lb_ref[...]
        for gi in range(A_GROUPS):
            cols = slice(gi * gw, (gi + 1) * gw)
            sp = _dot(ws_ref[gi].astype(BF16), v[:, cols].astype(BF16)) + bs_ref[gi]
            y_ref[rows, cols] = (u[:, cols] * sp).astype(BF16)


def _mixer_a(z, ln_g, ln_b, ws, bs, tm=256):
    m = z.shape[0]
    gw = A_WIDTH // A_GROUPS
    bs_b = jnp.broadcast_to(bs[:, :, None], (A_GROUPS, CHUNK, gw))
    return pl.pallas_call(
        functools.partial(_mixer_a_kernel, n_chunks=tm // CHUNK),
        grid=(m // tm,),
        in_specs=[
            pl.BlockSpec((tm, 2 * A_WIDTH), lambda i: (i, COL_A // (2 * A_WIDTH))),
            pl.BlockSpec((1, A_WIDTH), lambda i: (0, 0)),
            pl.BlockSpec((1, A_WIDTH), lambda i: (0, 0)),
            pl.BlockSpec((A_GROUPS, CHUNK, CHUNK), lambda i: (0, 0, 0)),
            pl.BlockSpec((A_GROUPS, CHUNK, gw), lambda i: (0, 0, 0)),
        ],
        out_specs=pl.BlockSpec((tm, A_WIDTH), lambda i: (i, 0)),
        out_shape=jax.ShapeDtypeStruct((m, A_WIDTH), BF16),
        compiler_params=_cparams(1, 32 * 2**20),
        name="mixer_a",
    )(z, ln_g.reshape(1, A_WIDTH), ln_b.reshape(1, A_WIDTH), ws, bs_b)


def _attn_ctx_kernel(q_ref, k_ref, v_ref, y_ref, ko_ref, vo_ref):
    q, k, v = q_ref[...], k_ref[...], v_ref[...]
    s = _dot_nt(q, k) * (NA_DH ** -0.5)
    m = jnp.max(s, axis=-1, keepdims=True)
    p = jnp.exp(s - m)
    l = jnp.sum(p, axis=-1, keepdims=True)
    o = _dot(p.astype(BF16), v) / l
    y_ref[...] = o.astype(BF16)
    ko_ref[...] = k.astype(F32)
    vo_ref[...] = v.astype(F32)


def _attn_ctx(z, n_ctx):
    qb, kb, vb = COL_QB // NA_DH, COL_KB // NA_DH, COL_VB // NA_DH
    cache = jax.ShapeDtypeStruct((n_ctx, NA_HEADS, SEQ, NA_DH), F32)
    cache_spec = pl.BlockSpec((None, None, SEQ, NA_DH), lambda b, h: (b, h, 0, 0))
    return pl.pallas_call(
        _attn_ctx_kernel,
        grid=(n_ctx, NA_HEADS),
        in_specs=[
            pl.BlockSpec((SEQ, NA_DH), lambda b, h: (b, qb + h)),
            pl.BlockSpec((SEQ, NA_DH), lambda b, h: (b, kb + h)),
            pl.BlockSpec((SEQ, NA_DH), lambda b, h: (b, vb + h)),
        ],
        out_specs=[pl.BlockSpec((SEQ, NA_DH), lambda b, h: (b, h)), cache_spec, cache_spec],
        out_shape=[jax.ShapeDtypeStruct((n_ctx * SEQ, NA_HEADS * NA_DH), BF16), cache, cache],
        compiler_params=_cparams(2, 16 * 2**20),
        name="attn_ctx",
    )(z, z, z)


NA_QROWS = 4
NA_KROWS = 12
NA_NQB = (DEC_SEQ // GRID_W) // NA_QROWS


def _na_block(qb):
    n_rows = DEC_SEQ // GRID_W
    k0 = min(max(NA_QROWS * qb - NA_WIN_H // 2, 0), n_rows - NA_KROWS)
    case = 0 if qb == 0 else (2 if qb == NA_NQB - 1 else 1)
    return k0, case


def _na_bias_tables(rpb):
    n_rows = DEC_SEQ // GRID_W
    dr = np.zeros((3, NA_QROWS, NA_KROWS), np.int32)
    ok_r = np.zeros((3, NA_QROWS, NA_KROWS), bool)
    for case, qb in enumerate((0, 1, NA_NQB - 1)):
        k0, _ = _na_block(qb)
        for qr in range(NA_QROWS):
            r = NA_QROWS * qb + qr
            rs = min(max(r - NA_WIN_H // 2, 0), n_rows - NA_WIN_H)
            for j in range(NA_KROWS):
                kr = k0 + j
                ok_r[case, qr, j] = rs <= kr < rs + NA_WIN_H
                dr[case, qr, j] = min(max(kr - r + NA_WIN_H - 1, 0), 2 * NA_WIN_H - 2)
    cols = np.arange(GRID_W)
    cs = np.clip(cols - NA_WIN_W // 2, 0, GRID_W - NA_WIN_W)
    ok_c = (cols[None, :] >= cs[:, None]) & (cols[None, :] < cs[:, None] + NA_WIN_W)
    dc = np.clip(cols[None, :] - cols[:, None], -(NA_WIN_W - 1), NA_WIN_W - 1) + NA_WIN_W - 1
    idx_r = dr[:, :, None, :, None]
    idx_c = dc[None, None, :, None, :]
    ok = ok_r[:, :, None, :, None] & ok_c[None, None, :, None, :]
    tab = jnp.where(ok[None], rpb[:, idx_r, idx_c], NEG)
    return tab.reshape(rpb.shape[0], 3, NA_QROWS * GRID_W, NA_KROWS * GRID_W).astype(F32)


def _attn_lat_kernel(q_ref, k_ref, v_ref, ck_ref, cv_ref, bias_ref, y_ref):
    scale = NA_DH ** -0.5
    ck = ck_ref[...].astype(BF16)
    cv = cv_ref[...].astype(BF16)
    nq = NA_QROWS * GRID_W
    nk = NA_KROWS * GRID_W
    for qb in range(NA_NQB):
        k0, case = _na_block(qb)
        q = q_ref[qb * nq:(qb + 1) * nq, :]
        kw = k_ref[k0 * GRID_W:k0 * GRID_W + nk, :]
        vw = v_ref[k0 * GRID_W:k0 * GRID_W + nk, :]
        sw = _dot_nt(q, kw) * scale + bias_ref[case]
        sc = _dot_nt(q, ck) * scale
        m = jnp.maximum(jnp.max(sw, axis=-1, keepdims=True), jnp.max(sc, axis=-1, keepdims=True))
        pw = jnp.exp(sw - m)
        pc = jnp.exp(sc - m)
        l = jnp.sum(pw, axis=-1, keepdims=True) + jnp.sum(pc, axis=-1, keepdims=True)
        o = _dot(pw.astype(BF16), vw) + _dot(pc.astype(BF16), cv)
        y_ref[qb * nq:(qb + 1) * nq, :] = (o / l).astype(BF16)


def _attn_lat(z, cache_k, cache_v, bias_tab, layer, n_lat, row_blk0):
    qb, kb, vb = COL_QB // NA_DH, COL_KB // NA_DH, COL_VB // NA_DH
    past = cache_k.shape[3]
    cspec = pl.BlockSpec((None, None, None, past, NA_DH), lambda b, h: (b, layer, h, 0, 0))
    return pl.pallas_call(
        _attn_lat_kernel,
        grid=(n_lat, NA_HEADS),
        in_specs=[
            pl.BlockSpec((DEC_SEQ, NA_DH), lambda b, h: (row_blk0 + b, qb + h)),
            pl.BlockSpec((DEC_SEQ, NA_DH), lambda b, h: (row_blk0 + b, kb + h)),
            pl.BlockSpec((DEC_SEQ, NA_DH), lambda b, h: (row_blk0 + b, vb + h)),
            cspec, cspec,
            pl.BlockSpec((None, 3, NA_QROWS * GRID_W, NA_KROWS * GRID_W), lambda b, h: (h, 0, 0, 0)),
        ],
        out_specs=pl.BlockSpec((DEC_SEQ, NA_DH), lambda b, h: (b, h)),
        out_shape=jax.ShapeDtypeStruct((n_lat * DEC_SEQ, NA_HEADS * NA_DH), BF16),
        compiler_params=_cparams(2, 40 * 2**20),
        name="attn_lat",
    )(z, z, z, cache_k, cache_v, bias_tab)


def _rope_tables(t_len):
    half = GLA_DK // 2
    nf = half // 2
    t = np.arange(t_len)
    inv = ROPE_BASE ** (-np.arange(nf, dtype=np.float32) / nf)
    ang_r = (t // GRID_W).astype(np.float32)[:, None] * inv
    ang_c = (t % GRID_W).astype(np.float32)[:, None] * inv
    cos = np.concatenate([np.cos(ang_r), np.cos(ang_r), np.cos(ang_c), np.cos(ang_c)], axis=-1)
    sin = np.concatenate([-np.sin(ang_r), np.sin(ang_r), -np.sin(ang_c), np.sin(ang_c)], axis=-1)
    return jnp.asarray(cos, F32), jnp.asarray(sin, F32)


def _log_sigmoid(x):
    return jnp.minimum(x, 0.0) - jnp.log(1.0 + jnp.exp(-jnp.abs(x)))


def _gla_kernel(*refs, t_len, rope, has_s0):
    refs = list(refs)
    q_ref, k_ref, v_ref, r_ref, lr_ref, wgf_ref, wgb_ref, bgf_ref, bgb_ref, ng_ref = refs[:10]
    pos = 10
    if rope:
        cos_ref, sin_ref = refs[pos:pos + 2]
        pos += 2
    if has_s0:
        s0_ref = refs[pos]
        pos += 1
    y_ref, sfin_ref, qs, ks, gfs, gbs, of = refs[pos:pos + 7]

    gc = GLA_CHUNK
    n_chunks = t_len // gc
    q = q_ref[...].astype(F32)
    k = k_ref[...].astype(F32)
    if rope:
        lane = lax.broadcasted_iota(jnp.int32, (t_len, GLA_DK), 1)
        first = (lane % (GLA_DK // 2)) < (GLA_DK // 4)

        def swap(x):
            return jnp.where(first, pltpu.roll(x, GLA_DK - GLA_DK // 4, 1), pltpu.roll(x, GLA_DK // 4, 1))

        q = q * cos_ref[...] + swap(q) * sin_ref[...]
        k = k * cos_ref[...] + swap(k) * sin_ref[...]
    qs[...] = q * (GLA_DK ** -0.5)
    ks[...] = k
    lrb = lr_ref[...].astype(BF16)
    inv_temp = 1.0 / GLA_GATE_TEMP
    gfs[...] = _log_sigmoid(_dot(lrb, wgf_ref[...].astype(BF16)) + bgf_ref[...]) * inv_temp
    gbs[...] = _log_sigmoid(_dot(lrb, wgb_ref[...].astype(BF16)) + bgb_ref[...]) * inv_temp

    row = lax.broadcasted_iota(jnp.int32, (gc, gc), 0)
    col = lax.broadcasted_iota(jnp.int32, (gc, gc), 1)
    mid = gc // 2

    def chunk(c, st, g_ref, causal):
        sl = pl.ds(pl.multiple_of(c * gc, gc), gc)
        mask = (row >= col) if causal else (row <= col)
        tri = mask.astype(BF16)
        qc, kc, vc, g = qs[sl, :], ks[sl, :], v_ref[sl, :], g_ref[sl, :]
        ghi, glo = _split_bf16(g)
        b = _dot(tri, ghi) + _dot(tri, glo)
        bmid = b[mid:mid + 1, :]
        tot = b[gc - 1:gc, :] if causal else b[0:1, :]
        q_in = qc * jnp.exp(b)
        qg = qc * jnp.exp(b - bmid)
        kg = kc * jnp.exp(bmid - b)
        kd = kc * jnp.exp(tot - b)
        sc = jnp.where(mask, _dot_nt(qg.astype(BF16), kg.astype(BF16)), 0.0)
        o = _dot(sc.astype(BF16), vc) + _dot_nt(q_in.astype(BF16), st.astype(BF16))
        st_new = jnp.exp(tot) * st + _dot_tn(vc, kd.astype(BF16))
        return o, st_new

    if has_s0:
        st_f0 = s0_ref[0].T
        st_b0 = s0_ref[1].T
    else:
        st_f0 = jnp.zeros((GLA_DV, GLA_DK), F32)
        st_b0 = st_f0

    def fwd(c, st):
        o, st = chunk(c, st, gfs, True)
        of[pl.ds(pl.multiple_of(c * gc, gc), gc), :] = o
        return st

    st_f = lax.fori_loop(0, n_chunks, fwd, st_f0)

    def bwd(i, st):
        c = n_chunks - 1 - i
        o, st = chunk(c, st, gbs, False)
        sl = pl.ds(pl.multiple_of(c * gc, gc), gc)
        o = o + of[sl, :]
        o = o * lax.rsqrt(jnp.mean(o * o, axis=-1, keepdims=True) + EPS) * ng_ref[...]
        r = r_ref[sl, :].astype(F32)
        y_ref[sl, :] = (o * (r * jax.nn.sigmoid(r))).astype(BF16)
        return st

    st_b = lax.fori_loop(0, n_chunks, bwd, st_b0)
    sfin_ref[0] = st_f.T
    sfin_ref[1] = st_b.T


def _gla(z, lr, wgf, wgb, bgf, bgb, norm_g, n_batch, t_len, row_blk0, name, rope_tabs=None, s0=None, layer=0):
    qc, kc = COL_QC // GLA_DK, COL_KC // GLA_DK
    vc, rc = COL_VC // GLA_DV, COL_RC // GLA_DV
    kdim = GLA_HEADS * GLA_DK
    in_specs = [
        pl.BlockSpec((t_len, GLA_DK), lambda b, h: (row_blk0 + b, qc + h)),
        pl.BlockSpec((t_len, GLA_DK), lambda b, h: (row_blk0 + b, kc + h)),
        pl.BlockSpec((t_len, GLA_DV), lambda b, h: (row_blk0 + b, vc + h)),
        pl.BlockSpec((t_len, GLA_DV), lambda b, h: (row_blk0 + b, rc + h)),
        pl.BlockSpec((t_len, 128), lambda b, h: (row_blk0 + b, 0)),
        pl.BlockSpec((128, GLA_DK), lambda b, h: (0, h)),
        pl.BlockSpec((128, GLA_DK), lambda b, h: (0, h)),
        pl.BlockSpec((1, GLA_DK), lambda b, h: (0, h)),
        pl.BlockSpec((1, GLA_DK), lambda b, h: (0, h)),
        pl.BlockSpec((1, GLA_DV), lambda b, h: (0, h)),
    ]
    args = [z, z, z, z, lr, wgf, wgb, bgf.reshape(1, kdim), bgb.reshape(1, kdim),
            norm_g.reshape(1, GLA_HEADS * GLA_DV)]
    if rope_tabs is not None:
        in_specs += [pl.BlockSpec((t_len, GLA_DK), lambda b, h: (0, 0))] * 2
        args += list(rope_tabs)
    if s0 is not None:
        in_specs.append(pl.BlockSpec((None, None, 2, None, GLA_DK, GLA_DV), lambda b, h: (b, layer, 0, h, 0, 0)))
        args.append(s0)
    return pl.pallas_call(
        functools.partial(_gla_kernel, t_len=t_len, rope=rope_tabs is not None, has_s0=s0 is not None),
        grid=(n_batch, GLA_HEADS),
        in_specs=in_specs,
        out_specs=[
            pl.BlockSpec((t_len, GLA_DV), lambda b, h: (b, h)),
            pl.BlockSpec((None, 2, None, GLA_DK, GLA_DV), lambda b, h: (b, 0, h, 0, 0)),
        ],
        out_shape=[
            jax.ShapeDtypeStruct((n_batch * t_len, GLA_HEADS * GLA_DV), BF16),
            jax.ShapeDtypeStruct((n_batch, 2, GLA_HEADS, GLA_DK, GLA_DV), F32),
        ],
        scratch_shapes=[pltpu.VMEM((t_len, GLA_DK), F32)] * 4 + [pltpu.VMEM((t_len, GLA_DV), F32)],
        compiler_params=_cparams(2, 40 * 2**20),
        name=name,
    )(*args)


def _merge_kernel(ya_ref, yb_ref, yc_ref, wa_ref, wb_ref, wc_ref, ga_ref, gb_ref, gc_ref, o_ref):
    m = jax.nn.sigmoid(ga_ref[...].astype(F32)) * _dot(ya_ref[...], wa_ref[...])
    m += jax.nn.sigmoid(gb_ref[...].astype(F32)) * _dot(yb_ref[...], wb_ref[...])
    m += jax.nn.sigmoid(gc_ref[...].astype(F32)) * _dot(yc_ref[...], wc_ref[...])
    o_ref[...] = m.astype(BF16)


def _merge(ya, yb, yc, wa, wb, wc, z, tm=512, tn=1024):
    m, kw = ya.shape
    d = wa.shape[1]
    y_spec = pl.BlockSpec((tm, kw), lambda i, j: (i, 0))
    w_spec = pl.BlockSpec((kw, tn), lambda i, j: (0, j))

    def gate_spec(col):
        return pl.BlockSpec((tm, tn), lambda i, j: (i, col // tn + j))

    return pl.pallas_call(
        _merge_kernel,
        grid=(m // tm, d // tn),
        in_specs=[y_spec, y_spec, y_spec, w_spec, w_spec, w_spec,
                  gate_spec(COL_GA), gate_spec(COL_GB), gate_spec(COL_GC)],
        out_specs=pl.BlockSpec((tm, tn), lambda i, j: (i, j)),
        out_shape=jax.ShapeDtypeStruct((m, d), BF16),
        compiler_params=_cparams(2, 48 * 2**20),
        name="merge",
    )(ya, yb, yc, wa, wb, wc, z, z, z)


def _extract_top(x, count, rows_out):
    n = x.shape[1]
    rid = lax.broadcasted_iota(jnp.int32, (rows_out, n), 0)
    vals = jnp.full((rows_out, n), NEG, F32)
    for j in range(count):
        m = jnp.max(x, axis=0, keepdims=True)
        vals = jnp.where(rid == j, m, vals)
        if j + 1 < count:
            x = jnp.where(x >= m, NEG, x)
    return vals


def _route_kernel(q_ref, k1_ref, k2_ref, th_ref, s2_ref, e2_ref, c1_ref):
    nk = PEER_NKEYS
    k = PEER_TOPK

    def scores(keys, qt):
        kh, kl = _split_bf16(keys)
        qh, ql = _split_bf16(qt)
        return _dot(kh, qh) + (_dot(kh, ql) + _dot(kl, qh))

    s1 = scores(k1_ref[...], q_ref[0:nk, :])
    s2 = scores(k2_ref[...], q_ref[nk:2 * nk, :])
    t1 = _extract_top(s1, k + 1, 24)
    t2 = _extract_top(s2, k + 1, 24)
    n = s1.shape[1]
    rid8 = lax.broadcasted_iota(jnp.int32, (8, n), 0)
    extra = jnp.where(rid8 == 0, t1[k:k + 1] + t2[0:1], jnp.where(rid8 == 1, t1[0:1] + t2[k:k + 1], NEG))
    cand = jnp.concatenate(
        [t1[0:1] + t2[0:k]] + [t1[a:a + 1] + t2[0:8] for a in range(1, k)] + [extra], axis=0)
    c = _extract_top(cand, k + 1, 24)
    tau = 0.5 * (c[k - 1:k] + c[k:k + 1])
    zsum = jnp.sum(jnp.exp(c[0:k] - c[0:1]), axis=0, keepdims=True)
    th_ref[...] = tau - s1
    s2_ref[...] = s2
    e2_ref[...] = jnp.exp(s2 - t2[0:1])
    c1_ref[...] = jnp.exp(s1 - t1[0:1]) / zsum


def _peer_route(qt, k1, k2, nt=256):
    m = qt.shape[1]
    nk = PEER_NKEYS
    out = jax.ShapeDtypeStruct((PEER_HEADS, nk, m), F32)
    ospec = pl.BlockSpec((None, nk, nt), lambda i, h: (h, 0, i))
    kspec = pl.BlockSpec((nk, nk), lambda i, h: (0, 0))
    return pl.pallas_call(
        _route_kernel,
        grid=(m // nt, PEER_HEADS),
        in_specs=[pl.BlockSpec((2 * nk, nt), lambda i, h: (h, i)), kspec, kspec],
        out_specs=[ospec] * 4,
        out_shape=[out] * 4,
        compiler_params=_cparams(2, 24 * 2**20),
        name="peer_route",
    )(qt, k1, k2)


def _peer_kernel(ht_ref, th_ref, s2_ref, e2_ref, c1_ref, u_ref, vt_ref, o_ref, act_ref, g_ref, *, te, nt):
    j = pl.program_id(1)
    nk = PEER_NKEYS
    groups = te // nk

    @pl.when(j == 0)
    def _():
        o_ref[...] = jnp.zeros_like(o_ref)

    act_ref[...] = _dot(u_ref[...], ht_ref[...])

    for r in range(groups):
        rows = slice(r * nk, (r + 1) * nk)
        for lb in range(nt // 128):
            lanes = slice(lb * 128, (lb + 1) * 128)
            w = jnp.zeros((nk, 128), F32)
            for h in range(PEER_HEADS):
                th = th_ref[h, r:r + 1, lanes]
                c1 = c1_ref[h, r:r + 1, lanes]
                w = w + jnp.where(s2_ref[h, :, lanes] >= th, e2_ref[h, :, lanes] * c1, 0.0)
            g_ref[rows, lanes] = (jax.nn.gelu(act_ref[rows, lanes]) * w).astype(BF16)

    o_ref[...] += _dot(vt_ref[...], g_ref[...])


def _peer(ht, th, s2, e2, c1, u, vt, nt=512, te=1024):
    d, m = ht.shape
    nk = PEER_NKEYS
    groups = te // nk
    assert groups == 8, "first-key rows of a tile must fill one sublane group"
    rspec = pl.BlockSpec((PEER_HEADS, nk, nt), lambda i, j: (0, 0, i))
    gspec = pl.BlockSpec((PEER_HEADS, groups, nt), lambda i, j: (0, j, i))
    est = (2 * (d * nt * 2 + 2 * PEER_HEADS * nk * nt * 4 + 2 * te * d * 2 + d * nt * 4)
           + te * nt * 6 + te * nt * 4 + d * nt * 4)
    return pl.pallas_call(
        functools.partial(_peer_kernel, te=te, nt=nt),
        grid=(m // nt, N_EXPERTS // te),
        in_specs=[
            pl.BlockSpec((d, nt), lambda i, j: (0, i)),
            gspec, rspec, rspec, gspec,
            pl.BlockSpec((te, d), lambda i, j: (j, 0)),
            pl.BlockSpec((d, te), lambda i, j: (0, j)),
        ],
        out_specs=pl.BlockSpec((d, nt), lambda i, j: (0, i)),
        out_shape=jax.ShapeDtypeStruct((d, m), F32),
        scratch_shapes=[pltpu.VMEM((te, nt), F32), pltpu.VMEM((te, nt), BF16)],
        compiler_params=_cparams(2, est + 6 * 2**20),
        name="peer_experts",
    )(ht, th, s2, e2, c1, u, vt)


def kernel(x_prompt, x_sample, cache_k, cache_v, state_gla, c, c_ctx, w_ada, b_ada, w_in, a_ln_g, a_ln_b, a_ws, a_bs, na_rpb, gla_wg_f, gla_bg_f, gla_wg_b, gla_bg_b, gla_norm_g, w_br_a, w_br_b, w_br_c, w_out, ln1_g, ln1_b, ln2_g, ln2_b, peer_wq, peer_k1, peer_k2, peer_u, peer_v):
    n_ctx, n_lat = x_prompt.shape[0], x_sample.shape[0]
    n_layers = w_in.shape[0]
    d = D_MODEL
    mc, ml = n_ctx * SEQ, n_lat * DEC_SEQ
    assert mc % DEC_SEQ == 0, "latent row blocks must start on a DEC_SEQ boundary of the shared token axis"
    assert n_lat + 1 <= 16

    cc = jnp.zeros((16, d), F32).at[0].set(c_ctx).at[1:1 + n_lat].set(c)
    mod = _ada(cc, w_ada, b_ada).reshape(n_layers, 16, N_MOD, 1, d)

    def mod_vec(layer, which):
        return mod[layer, :, which]

    w_main = jnp.concatenate([w_in[:, :, :LR_START], w_in[:, :, LR_END:]], axis=-1).astype(BF16)
    w_lr = jnp.pad(w_in[:, :, LR_START:LR_END], ((0, 0), (0, 0), (0, 128 - (LR_END - LR_START)))).astype(BF16)
    wgf = jnp.pad(gla_wg_f, ((0, 0), (0, 128 - GLA_RANK), (0, 0)))
    wgb = jnp.pad(gla_wg_b, ((0, 0), (GLA_RANK, 128 - 2 * GLA_RANK), (0, 0)))
    wbr_a, wbr_b, wbr_c = w_br_a.astype(BF16), w_br_b.astype(BF16), w_br_c.astype(BF16)
    w_o = w_out.astype(BF16)
    wq_t = jnp.swapaxes(peer_wq, 1, 2).astype(BF16)
    u_b = peer_u.astype(BF16)
    v_t = jnp.swapaxes(peer_v, 1, 2).astype(BF16)
    rope_tabs = _rope_tables(DEC_SEQ)

    x = jnp.concatenate([x_prompt.reshape(mc, d), x_sample.reshape(ml, d)], axis=0)
    h = _modulate(x, mod_vec(0, 0), mod_vec(0, 1), mc)

    new_k, new_v, new_s = [], [], []
    for l in range(n_layers):
        z = _matmul(h, w_main[l], BF16, 512, 2048, "in_proj", b_resident=True)
        lr = _matmul(h, w_lr[l], F32, 512, 128, "lr_proj")
        ya = _mixer_a(z, a_ln_g[l], a_ln_b[l], a_ws[l], a_bs[l])
        yb_c, k_l, v_l = _attn_ctx(z, n_ctx)
        yb_l = _attn_lat(z, cache_k, cache_v, _na_bias_tables(na_rpb[l]), l, n_lat, mc // DEC_SEQ)
        gla_args = (wgf[l], wgb[l], gla_bg_f[l], gla_bg_b[l], gla_norm_g[l])
        yc_c, s_l = _gla(z, lr, *gla_args, n_ctx, SEQ, 0, "gla_ctx")
        yc_l, _ = _gla(z, lr, *gla_args, n_lat, DEC_SEQ, mc // DEC_SEQ, "gla_lat",
                       rope_tabs=rope_tabs, s0=state_gla, layer=l)
        new_k.append(k_l)
        new_v.append(v_l)
        new_s.append(s_l)
        yb = jnp.concatenate([yb_c, yb_l], axis=0)
        yc = jnp.concatenate([yc_c, yc_l], axis=0)
        mrg = _merge(ya, yb, yc, wbr_a[l], wbr_b[l], wbr_c[l], z)
        y = _matmul(mrg, w_o[l], F32, 512, 1024, "out_proj")
        x, h2 = _res_ln(x, y, mod_vec(l, 2), ln1_g[l], ln1_b[l], mod_vec(l, 3), mod_vec(l, 4), mc, "res_ln1")
        h2t = h2.T
        qt = _matmul(wq_t[l], h2t, F32, 512, 1024, "peer_q")
        th, s2, e2, c1 = _peer_route(qt, peer_k1[l], peer_k2[l])
        pt = _peer(h2t, th, s2, e2, c1, u_b[l], v_t[l])
        nxt = min(l + 1, n_layers - 1)
        x, h = _res_ln(x, pt.T, mod_vec(l, 5), ln2_g[l], ln2_b[l], mod_vec(nxt, 0), mod_vec(nxt, 1), mc, "res_ln2")

    y_prompt = x[:mc].reshape(n_ctx, SEQ, d)
    y_sample = x[mc:].reshape(n_lat, DEC_SEQ, d)
    return (y_prompt, y_sample, jnp.stack(new_k, axis=1), jnp.stack(new_v, axis=1), jnp.stack(new_s, axis=1))
```

```python
import functools

import numpy as np
import jax
import jax.numpy as jnp
from jax import lax
from jax.experimental import pallas as pl
from jax.experimental.pallas import tpu as pltpu

F32 = jnp.float32
BF16 = jnp.bfloat16

D_MODEL = 2048
SEQ = 256
DEC_SEQ = 2048
GRID_W = 64
CHUNK = 128
A_GROUPS = 8
A_WIDTH = 1024
NA_HEADS = 8
NA_DH = 128
NA_WIN_H = 8
NA_WIN_W = 16
GLA_HEADS = 4
GLA_DK = 128
GLA_DV = 256
GLA_RANK = 16
GLA_GATE_TEMP = 16.0
GLA_CHUNK = 64
ROPE_BASE = 10000.0
PEER_HEADS = 8
PEER_NKEYS = 128
PEER_TOPK = 16
N_EXPERTS = PEER_NKEYS * PEER_NKEYS
N_MOD = 6
ALPHA = 8.0 ** 0.25
EPS = 1e-5
NEG = -1e30

Z_WIDTH = 14336
COL_A = 0
COL_QB, COL_KB, COL_VB = 2048, 3072, 4096
COL_QC, COL_KC, COL_VC, COL_RC = 5120, 5632, 6144, 7168
COL_GA, COL_GB, COL_GC = 8192, 10240, 12288
LR_START, LR_END = 8192, 8224

VMEM_CAP = 56 * 1024 * 1024


def _cparams(n_axes, vmem_bytes, flags=None):
    return pltpu.CompilerParams(
        dimension_semantics=("arbitrary",) * n_axes,
        vmem_limit_bytes=min(int(vmem_bytes), VMEM_CAP),
        flags=flags,
    )


def _mod_row(i, tm, n_ctx_rows):
    start = i * tm
    return jnp.where(start < n_ctx_rows, 0, 1 + (start - n_ctx_rows) // DEC_SEQ)


def _dot(a, b):
    return jnp.dot(a, b, preferred_element_type=F32)


def _dot_nt(a, b):
    return lax.dot_general(a, b, (((1,), (1,)), ((), ())), preferred_element_type=F32)


def _dot_tn(a, b):
    return lax.dot_general(a, b, (((0,), (0,)), ((), ())), preferred_element_type=F32)


def _split_bf16(x):
    hi = x.astype(BF16)
    lo = (x - hi.astype(F32)).astype(BF16)
    return hi, lo


def _ada_kernel(c_ref, w_ref, b_ref, o_ref):
    c = c_ref[...]
    a = (c * jax.nn.sigmoid(c)).astype(BF16)
    o_ref[...] = _dot(a, w_ref[...].astype(BF16)) + b_ref[...]


def _ada(cc, w_ada, b_ada):
    n_layers, d, n = w_ada.shape
    rows = cc.shape[0]
    tn = 1024
    return pl.pallas_call(
        _ada_kernel,
        grid=(n_layers, n // tn),
        in_specs=[
            pl.BlockSpec((rows, d), lambda l, j: (0, 0)),
            pl.BlockSpec((None, d, tn), lambda l, j: (l, 0, j)),
            pl.BlockSpec((None, 1, tn), lambda l, j: (l, 0, j)),
        ],
        out_specs=pl.BlockSpec((None, rows, tn), lambda l, j: (l, 0, j)),
        out_shape=jax.ShapeDtypeStruct((n_layers, rows, n), F32),
        compiler_params=_cparams(2, 2 * d * tn * 4 + 12 * 2**20),
        name="ada_mod",
    )(cc, w_ada, b_ada.reshape(n_layers, 1, n))


def _mm_kernel(a_ref, b_ref, o_ref):
    o_ref[...] = _dot(a_ref[...].astype(BF16), b_ref[...].astype(BF16)).astype(o_ref.dtype)


def _matmul(a, b, out_dtype, tm, tn, name, b_resident=False):
    m, k = a.shape
    _, n = b.shape
    assert m % tm == 0 and n % tn == 0, (m, n, tm, tn)
    est = 2 * (tm * k * a.dtype.itemsize + k * tn * b.dtype.itemsize + tm * tn * 4) + tm * tn * 8
    if b_resident:
        grid = (n // tn, m // tm)
        a_map, b_map, o_map = (lambda j, i: (i, 0)), (lambda j, i: (0, j)), (lambda j, i: (i, j))
    else:
        grid = (m // tm, n // tn)
        a_map, b_map, o_map = (lambda i, j: (i, 0)), (lambda i, j: (0, j)), (lambda i, j: (i, j))
    return pl.pallas_call(
        _mm_kernel,
        grid=grid,
        in_specs=[pl.BlockSpec((tm, k), a_map), pl.BlockSpec((k, tn), b_map)],
        out_specs=pl.BlockSpec((tm, tn), o_map),
        out_shape=jax.ShapeDtypeStruct((m, n), out_dtype),
        compiler_params=_cparams(2, est + 8 * 2**20),
        name=name,
    )(a, b)


def _modulate_kernel(x_ref, sh_ref, sc_ref, h_ref):
    h_ref[...] = (x_ref[...] * (1.0 + sc_ref[...]) + sh_ref[...]).astype(BF16)


def _modulate(x, sh, sc, n_ctx_rows, tm=256):
    m, d = x.shape
    vec = pl.BlockSpec((None, 1, d), lambda i: (_mod_row(i, tm, n_ctx_rows), 0, 0))
    return pl.pallas_call(
        _modulate_kernel,
        grid=(m // tm,),
        in_specs=[pl.BlockSpec((tm, d), lambda i: (i, 0)), vec, vec],
        out_specs=pl.BlockSpec((tm, d), lambda i: (i, 0)),
        out_shape=jax.ShapeDtypeStruct((m, d), BF16),
        compiler_params=_cparams(1, 32 * 2**20),
        name="modulate",
    )(x, sh, sc)


def _res_ln_kernel(x_ref, y_ref, g_ref, lg_ref, lb_ref, sh_ref, sc_ref, xo_ref, h_ref):
    t = ALPHA * x_ref[...] + g_ref[...] * y_ref[...].astype(F32)
    mu = jnp.mean(t, axis=-1, keepdims=True)
    tc = t - mu
    var = jnp.mean(tc * tc, axis=-1, keepdims=True)
    xn = tc * lax.rsqrt(var + EPS) * lg_ref[...] + lb_ref[...]
    xo_ref[...] = xn
    h_ref[...] = (xn * (1.0 + sc_ref[...]) + sh_ref[...]).astype(BF16)


def _res_ln(x, y, g, ln_g, ln_b, sh, sc, n_ctx_rows, name, tm=256):
    m, d = x.shape
    vec = pl.BlockSpec((None, 1, d), lambda i: (_mod_row(i, tm, n_ctx_rows), 0, 0))
    par = pl.BlockSpec((1, d), lambda i: (0, 0))
    row = pl.BlockSpec((tm, d), lambda i: (i, 0))
    return pl.pallas_call(
        _res_ln_kernel,
        grid=(m // tm,),
        in_specs=[row, row, vec, par, par, vec, vec],
        out_specs=[row, row],
        out_shape=[jax.ShapeDtypeStruct((m, d), F32), jax.ShapeDtypeStruct((m, d), BF16)],
        compiler_params=_cparams(1, 40 * 2**20),
        name=name,
    )(x, y, g, ln_g.reshape(1, d), ln_b.reshape(1, d), sh, sc)


def _mixer_a_kernel(a_ref, lg_ref, lb_ref, ws_ref, bs_ref, y_ref, *, n_chunks):
    gw = A_WIDTH // A_GROUPS
    for c in range(n_chunks):
        rows = slice(c * CHUNK, (c + 1) * CHUNK)
        g = jax.nn.gelu(a_ref[rows, :].astype(F32))
        u = g[:, :A_WIDTH]
        v = g[:, A_WIDTH:]
        mu = jnp.mean(v, axis=-1, keepdims=True)
        vc = v - mu
        var = jnp.mean(vc * vc, axis=-1, keepdims=True)
        v = vc * lax.rsqrt(var + EPS) * lg_ref[...] + lb_ref[...]
        for gi in range(A_GROUPS):
            cols = slice(gi * gw, (gi + 1) * gw)
            sp = _dot(ws_ref[gi].astype(BF16), v[:, cols].astype(BF16)) + bs_ref[gi]
            y_ref[rows, cols] = (u[:, cols] * sp).astype(BF16)


def _mixer_a(z, ln_g, ln_b, ws, bs, tm=256):
    m = z.shape[0]
    gw = A_WIDTH // A_GROUPS
    bs_b = jnp.broadcast_to(bs[:, :, None], (A_GROUPS, CHUNK, gw))
    return pl.pallas_call(
        functools.partial(_mixer_a_kernel, n_chunks=tm // CHUNK),
        grid=(m // tm,),
        in_specs=[
            pl.BlockSpec((tm, 2 * A_WIDTH), lambda i: (i, COL_A // (2 * A_WIDTH))),
            pl.BlockSpec((1, A_WIDTH), lambda i: (0, 0)),
            pl.BlockSpec((1, A_WIDTH), lambda i: (0, 0)),
            pl.BlockSpec((A_GROUPS, CHUNK, CHUNK), lambda i: (0, 0, 0)),
            pl.BlockSpec((A_GROUPS, CHUNK, gw), lambda i: (0, 0, 0)),
        ],
        out_specs=pl.BlockSpec((tm, A_WIDTH), lambda i: (i, 0)),
        out_shape=jax.ShapeDtypeStruct((m, A_WIDTH), BF16),
        compiler_params=_cparams(1, 32 * 2**20),
        name="mixer_a",
    )(z, ln_g.reshape(1, A_WIDTH), ln_b.reshape(1, A_WIDTH), ws, bs_b)


def _attn_ctx_kernel(q_ref, k_ref, v_ref, y_ref, ko_ref, vo_ref):
    q, k, v = q_ref[...], k_ref[...], v_ref[...]
    s = _dot_nt(q, k) * (NA_DH ** -0.5)
    m = jnp.max(s, axis=-1, keepdims=True)
    p = jnp.exp(s - m)
    l = jnp.sum(p, axis=-1, keepdims=True)
    o = _dot(p.astype(BF16), v) / l
    y_ref[...] = o.astype(BF16)
    ko_ref[...] = k.astype(F32)
    vo_ref[...] = v.astype(F32)


def _attn_ctx(z, n_ctx):
    qb, kb, vb = COL_QB // NA_DH, COL_KB // NA_DH, COL_VB // NA_DH
    cache = jax.ShapeDtypeStruct((n_ctx, NA_HEADS, SEQ, NA_DH), F32)
    cache_spec = pl.BlockSpec((None, None, SEQ, NA_DH), lambda b, h: (b, h, 0, 0))
    return pl.pallas_call(
        _attn_ctx_kernel,
        grid=(n_ctx, NA_HEADS),
        in_specs=[
            pl.BlockSpec((SEQ, NA_DH), lambda b, h: (b, qb + h)),
            pl.BlockSpec((SEQ, NA_DH), lambda b, h: (b, kb + h)),
            pl.BlockSpec((SEQ, NA_DH), lambda b, h: (b, vb + h)),
        ],
        out_specs=[pl.BlockSpec((SEQ, NA_DH), lambda b, h: (b, h)), cache_spec, cache_spec],
        out_shape=[jax.ShapeDtypeStruct((n_ctx * SEQ, NA_HEADS * NA_DH), BF16), cache, cache],
        compiler_params=_cparams(2, 16 * 2**20),
        name="attn_ctx",
    )(z, z, z)


NA_QROWS = 4
NA_KROWS = 12
NA_NQB = (DEC_SEQ // GRID_W) // NA_QROWS


def _na_block(qb):
    n_rows = DEC_SEQ // GRID_W
    k0 = min(max(NA_QROWS * qb - NA_WIN_H // 2, 0), n_rows - NA_KROWS)
    case = 0 if qb == 0 else (2 if qb == NA_NQB - 1 else 1)
    return k0, case


def _rpb_expand_kernel(rp_ref, oh_ref, neg_ref, o_ref):
    r = rp_ref[...]
    hi = r.astype(BF16)
    r1 = r - hi.astype(F32)
    mid = r1.astype(BF16)
    lo = (r1 - mid.astype(F32)).astype(BF16)
    oh = oh_ref[...]
    o_ref[...] = (_dot(hi, oh) + _dot(mid, oh)) + (_dot(lo, oh) + neg_ref[...])


def _na_bias_tables(rpb):
    n_layers, n_heads, n_dr, n_dc = rpb.shape
    cols = np.arange(GRID_W)
    cs = np.clip(cols - NA_WIN_W // 2, 0, GRID_W - NA_WIN_W)
    ok_c = (cols[None, :] >= cs[:, None]) & (cols[None, :] < cs[:, None] + NA_WIN_W)
    dc = np.clip(cols[None, :] - cols[:, None], -(NA_WIN_W - 1), NA_WIN_W - 1) + NA_WIN_W - 1
    onehot = (np.arange(32)[:, None] == dc.reshape(1, -1)).astype(np.float32)
    negmask = np.where(ok_c.reshape(1, -1), 0.0, NEG).astype(np.float32)
    rows = n_layers * n_heads * n_dr
    rows_pad = -(-rows // 128) * 128
    rp = jnp.pad(rpb.reshape(rows, n_dc), ((0, rows_pad - rows), (0, 32 - n_dc)))
    t = pl.pallas_call(
        _rpb_expand_kernel,
        grid=(rows_pad // 128,),
        in_specs=[pl.BlockSpec((128, 32), lambda i: (i, 0)),
                  pl.BlockSpec((32, GRID_W * GRID_W), lambda i: (0, 0)),
                  pl.BlockSpec((1, GRID_W * GRID_W), lambda i: (0, 0))],
        out_specs=pl.BlockSpec((128, GRID_W * GRID_W), lambda i: (i, 0)),
        out_shape=jax.ShapeDtypeStruct((rows_pad, GRID_W * GRID_W), F32),
        compiler_params=_cparams(1, 16 * 2**20),
        name="rpb_expand",
    )(rp, jnp.asarray(onehot, BF16), jnp.asarray(negmask))
    t = t[:rows].reshape(n_layers, n_heads, n_dr, GRID_W, GRID_W)
    n_rows = DEC_SEQ // GRID_W
    neg_blk = jnp.full((n_layers, n_heads, GRID_W, GRID_W), NEG, F32)
    cases = []
    for qb in (0, 1, NA_NQB - 1):
        k0, _ = _na_block(qb)
        q_rows = []
        for qr in range(NA_QROWS):
            r = NA_QROWS * qb + qr
            rs = min(max(r - NA_WIN_H // 2, 0), n_rows - NA_WIN_H)
            blks = []
            for j in range(NA_KROWS):
                kr = k0 + j
                blks.append(t[:, :, kr - r + NA_WIN_H - 1] if rs <= kr < rs + NA_WIN_H else neg_blk)
            q_rows.append(jnp.concatenate(blks, axis=-1))
        cases.append(jnp.concatenate(q_rows, axis=-2))
    return jnp.stack(cases, axis=2)


def _attn_lat_kernel(q_ref, k_ref, v_ref, ck_ref, cv_ref, bias_ref, y_ref):
    scale = NA_DH ** -0.5
    ck = ck_ref[...].astype(BF16)
    cv = cv_ref[...].astype(BF16)
    nq = NA_QROWS * GRID_W
    nk = NA_KROWS * GRID_W
    for qb in range(NA_NQB):
        k0, case = _na_block(qb)
        q = q_ref[qb * nq:(qb + 1) * nq, :]
        kw = k_ref[k0 * GRID_W:k0 * GRID_W + nk, :]
        vw = v_ref[k0 * GRID_W:k0 * GRID_W + nk, :]
        sw = _dot_nt(q, kw) * scale + bias_ref[case]
        sc = _dot_nt(q, ck) * scale
        m = jnp.maximum(jnp.max(sw, axis=-1, keepdims=True), jnp.max(sc, axis=-1, keepdims=True))
        pw = jnp.exp(sw - m)
        pc = jnp.exp(sc - m)
        l = jnp.sum(pw, axis=-1, keepdims=True) + jnp.sum(pc, axis=-1, keepdims=True)
        o = _dot(pw.astype(BF16), vw) + _dot(pc.astype(BF16), cv)
        y_ref[qb * nq:(qb + 1) * nq, :] = (o / l).astype(BF16)


def _attn_lat(z, cache_k, cache_v, bias_tab, layer, n_lat, row_blk0):
    qb, kb, vb = COL_QB // NA_DH, COL_KB // NA_DH, COL_VB // NA_DH
    past = cache_k.shape[3]
    cspec = pl.BlockSpec((None, None, None, past, NA_DH), lambda b, h: (b, layer, h, 0, 0))
    return pl.pallas_call(
        _attn_lat_kernel,
        grid=(n_lat, NA_HEADS),
        in_specs=[
            pl.BlockSpec((DEC_SEQ, NA_DH), lambda b, h: (row_blk0 + b, qb + h)),
            pl.BlockSpec((DEC_SEQ, NA_DH), lambda b, h: (row_blk0 + b, kb + h)),
            pl.BlockSpec((DEC_SEQ, NA_DH), lambda b, h: (row_blk0 + b, vb + h)),
            cspec, cspec,
            pl.BlockSpec((None, None, 3, NA_QROWS * GRID_W, NA_KROWS * GRID_W), lambda b, h: (layer, h, 0, 0, 0)),
        ],
        out_specs=pl.BlockSpec((DEC_SEQ, NA_DH), lambda b, h: (b, h)),
        out_shape=jax.ShapeDtypeStruct((n_lat * DEC_SEQ, NA_HEADS * NA_DH), BF16),
        compiler_params=_cparams(2, 40 * 2**20),
        name="attn_lat",
    )(z, z, z, cache_k, cache_v, bias_tab)


def _rope_tables(t_len):
    half = GLA_DK // 2
    nf = half // 2
    t = np.arange(t_len)
    inv = ROPE_BASE ** (-np.arange(nf, dtype=np.float32) / nf)
    ang_r = (t // GRID_W).astype(np.float32)[:, None] * inv
    ang_c = (t % GRID_W).astype(np.float32)[:, None] * inv
    cos = np.concatenate([np.cos(ang_r), np.cos(ang_r), np.cos(ang_c), np.cos(ang_c)], axis=-1)
    sin = np.concatenate([-np.sin(ang_r), np.sin(ang_r), -np.sin(ang_c), np.sin(ang_c)], axis=-1)
    return jnp.asarray(cos, F32), jnp.asarray(sin, F32)


def _log_sigmoid(x):
    return jnp.minimum(x, 0.0) - jnp.log(1.0 + jnp.exp(-jnp.abs(x)))


def _gla_kernel(*refs, t_len, rope, has_s0):
    refs = list(refs)
    q_ref, k_ref, v_ref, r_ref, lr_ref, wgf_ref, wgb_ref, bgf_ref, bgb_ref, ng_ref = refs[:10]
    pos = 10
    if rope:
        cos_ref, sin_ref = refs[pos:pos + 2]
        pos += 2
    if has_s0:
        s0_ref = refs[pos]
        pos += 1
    y_ref, sfin_ref, qs, ks, gfs, gbs, of, ob = refs[pos:pos + 8]

    gc = GLA_CHUNK
    n_chunks = t_len // gc
    q = q_ref[...].astype(F32)
    k = k_ref[...].astype(F32)
    if rope:
        lane = lax.broadcasted_iota(jnp.int32, (t_len, GLA_DK), 1)
        first = (lane % (GLA_DK // 2)) < (GLA_DK // 4)

        def swap(x):
            return jnp.where(first, pltpu.roll(x, GLA_DK - GLA_DK // 4, 1), pltpu.roll(x, GLA_DK // 4, 1))

        q = q * cos_ref[...] + swap(q) * sin_ref[...]
        k = k * cos_ref[...] + swap(k) * sin_ref[...]
    qs[...] = q * (GLA_DK ** -0.5)
    ks[...] = k
    lrb = lr_ref[...].astype(BF16)
    inv_temp = 1.0 / GLA_GATE_TEMP
    gfs[...] = _log_sigmoid(_dot(lrb, wgf_ref[...].astype(BF16)) + bgf_ref[...]) * inv_temp
    gbs[...] = _log_sigmoid(_dot(lrb, wgb_ref[...].astype(BF16)) + bgb_ref[...]) * inv_temp

    row = lax.broadcasted_iota(jnp.int32, (gc, gc), 0)
    col = lax.broadcasted_iota(jnp.int32, (gc, gc), 1)
    mid = gc // 2

    def chunk(c, st, g_ref, causal):
        sl = pl.ds(pl.multiple_of(c * gc, gc), gc)
        mask = (row >= col) if causal else (row <= col)
        tri = mask.astype(BF16)
        qc, kc, vc, g = qs[sl, :], ks[sl, :], v_ref[sl, :], g_ref[sl, :]
        ghi, glo = _split_bf16(g)
        b = _dot(tri, ghi) + _dot(tri, glo)
        bmid = b[mid:mid + 1, :]
        tot = b[gc - 1:gc, :] if causal else b[0:1, :]
        q_in = qc * jnp.exp(b)
        qg = qc * jnp.exp(b - bmid)
        kg = kc * jnp.exp(bmid - b)
        kd = kc * jnp.exp(tot - b)
        sc = jnp.where(mask, _dot_nt(qg.astype(BF16), kg.astype(BF16)), 0.0)
        o = _dot(sc.astype(BF16), vc) + _dot_nt(q_in.astype(BF16), st.astype(BF16))
        st_new = jnp.exp(tot) * st + _dot_tn(vc, kd.astype(BF16))
        return o, st_new

    if has_s0:
        st_f0 = s0_ref[0].T
        st_b0 = s0_ref[1].T
    else:
        st_f0 = jnp.zeros((GLA_DV, GLA_DK), F32)
        st_b0 = st_f0

    def scan_step(i, carry):
        st_f, st_b = carry
        cb = n_chunks - 1 - i
        o_f, st_f = chunk(i, st_f, gfs, True)
        o_b, st_b = chunk(cb, st_b, gbs, False)
        of[pl.ds(pl.multiple_of(i * gc, gc), gc), :] = o_f
        ob[pl.ds(pl.multiple_of(cb * gc, gc), gc), :] = o_b
        return st_f, st_b

    st_f, st_b = lax.fori_loop(0, n_chunks, scan_step, (st_f0, st_b0))
    sfin_ref[0] = st_f.T
    sfin_ref[1] = st_b.T

    def finish(c, carry):
        sl = pl.ds(pl.multiple_of(c * gc, gc), gc)
        o = of[sl, :] + ob[sl, :]
        o = o * lax.rsqrt(jnp.mean(o * o, axis=-1, keepdims=True) + EPS) * ng_ref[...]
        r = r_ref[sl, :].astype(F32)
        y_ref[sl, :] = (o * (r * jax.nn.sigmoid(r))).astype(BF16)
        return carry

    lax.fori_loop(0, n_chunks, finish, 0)


def _gla(z, lr, wgf, wgb, bgf, bgb, norm_g, n_batch, t_len, row_blk0, name, rope_tabs=None, s0=None, layer=0):
    qc, kc = COL_QC // GLA_DK, COL_KC // GLA_DK
    vc, rc = COL_VC // GLA_DV, COL_RC // GLA_DV
    kdim = GLA_HEADS * GLA_DK
    in_specs = [
        pl.BlockSpec((t_len, GLA_DK), lambda b, h: (row_blk0 + b, qc + h)),
        pl.BlockSpec((t_len, GLA_DK), lambda b, h: (row_blk0 + b, kc + h)),
        pl.BlockSpec((t_len, GLA_DV), lambda b, h: (row_blk0 + b, vc + h)),
        pl.BlockSpec((t_len, GLA_DV), lambda b, h: (row_blk0 + b, rc + h)),
        pl.BlockSpec((t_len, 128), lambda b, h: (row_blk0 + b, 0)),
        pl.BlockSpec((128, GLA_DK), lambda b, h: (0, h)),
        pl.BlockSpec((128, GLA_DK), lambda b, h: (0, h)),
        pl.BlockSpec((1, GLA_DK), lambda b, h: (0, h)),
        pl.BlockSpec((1, GLA_DK), lambda b, h: (0, h)),
        pl.BlockSpec((1, GLA_DV), lambda b, h: (0, h)),
    ]
    args = [z, z, z, z, lr, wgf, wgb, bgf.reshape(1, kdim), bgb.reshape(1, kdim),
            norm_g.reshape(1, GLA_HEADS * GLA_DV)]
    if rope_tabs is not None:
        in_specs += [pl.BlockSpec((t_len, GLA_DK), lambda b, h: (0, 0))] * 2
        args += list(rope_tabs)
    if s0 is not None:
        in_specs.append(pl.BlockSpec((None, None, 2, None, GLA_DK, GLA_DV), lambda b, h: (b, layer, 0, h, 0, 0)))
        args.append(s0)
    return pl.pallas_call(
        functools.partial(_gla_kernel, t_len=t_len, rope=rope_tabs is not None, has_s0=s0 is not None),
        grid=(n_batch, GLA_HEADS),
        in_specs=in_specs,
        out_specs=[
            pl.BlockSpec((t_len, GLA_DV), lambda b, h: (b, h)),
            pl.BlockSpec((None, 2, None, GLA_DK, GLA_DV), lambda b, h: (b, 0, h, 0, 0)),
        ],
        out_shape=[
            jax.ShapeDtypeStruct((n_batch * t_len, GLA_HEADS * GLA_DV), BF16),
            jax.ShapeDtypeStruct((n_batch, 2, GLA_HEADS, GLA_DK, GLA_DV), F32),
        ],
        scratch_shapes=[pltpu.VMEM((t_len, GLA_DK), F32)] * 4 + [pltpu.VMEM((t_len, GLA_DV), F32)] * 2,
        compiler_params=_cparams(2, 40 * 2**20),
        name=name,
    )(*args)


def _merge_kernel(ya_ref, yb_ref, yc_ref, wa_ref, wb_ref, wc_ref, ga_ref, gb_ref, gc_ref, o_ref):
    m = jax.nn.sigmoid(ga_ref[...].astype(F32)) * _dot(ya_ref[...], wa_ref[...])
    m += jax.nn.sigmoid(gb_ref[...].astype(F32)) * _dot(yb_ref[...], wb_ref[...])
    m += jax.nn.sigmoid(gc_ref[...].astype(F32)) * _dot(yc_ref[...], wc_ref[...])
    o_ref[...] = m.astype(BF16)


def _merge(ya, yb, yc, wa, wb, wc, z, tm=512, tn=1024):
    m, kw = ya.shape
    d = wa.shape[1]
    y_spec = pl.BlockSpec((tm, kw), lambda i, j: (i, 0))
    w_spec = pl.BlockSpec((kw, tn), lambda i, j: (0, j))

    def gate_spec(col):
        return pl.BlockSpec((tm, tn), lambda i, j: (i, col // tn + j))

    return pl.pallas_call(
        _merge_kernel,
        grid=(m // tm, d // tn),
        in_specs=[y_spec, y_spec, y_spec, w_spec, w_spec, w_spec,
                  gate_spec(COL_GA), gate_spec(COL_GB), gate_spec(COL_GC)],
        out_specs=pl.BlockSpec((tm, tn), lambda i, j: (i, j)),
        out_shape=jax.ShapeDtypeStruct((m, d), BF16),
        compiler_params=_cparams(2, 48 * 2**20),
        name="merge",
    )(ya, yb, yc, wa, wb, wc, z, z, z)


def _extract_top(x, count, rows_out):
    n = x.shape[1]
    rid = lax.broadcasted_iota(jnp.int32, (rows_out, n), 0)
    vals = jnp.full((rows_out, n), NEG, F32)
    for j in range(count):
        m = jnp.max(x, axis=0, keepdims=True)
        vals = jnp.where(rid == j, m, vals)
        if j + 1 < count:
            x = jnp.where(x >= m, NEG, x)
    return vals


def _route_kernel(q_ref, k1_ref, k2_ref, th_ref, s2_ref, e2_ref, c1_ref):
    nk = PEER_NKEYS
    k = PEER_TOPK

    def scores(keys, qt):
        kh, kl = _split_bf16(keys)
        qh, ql = _split_bf16(qt)
        return _dot(kh, qh) + (_dot(kh, ql) + _dot(kl, qh))

    s1 = scores(k1_ref[...], q_ref[0:nk, :])
    s2 = scores(k2_ref[...], q_ref[nk:2 * nk, :])
    t1 = _extract_top(s1, k + 1, 24)
    t2 = _extract_top(s2, k + 1, 24)
    n = s1.shape[1]
    rid8 = lax.broadcasted_iota(jnp.int32, (8, n), 0)
    extra = jnp.where(rid8 == 0, t1[k:k + 1] + t2[0:1], jnp.where(rid8 == 1, t1[0:1] + t2[k:k + 1], NEG))
    cand = jnp.concatenate(
        [t1[0:1] + t2[0:k]] + [t1[a:a + 1] + t2[0:8] for a in range(1, k)] + [extra], axis=0)
    c = _extract_top(cand, k + 1, 24)
    tau = 0.5 * (c[k - 1:k] + c[k:k + 1])
    zsum = jnp.sum(jnp.exp(c[0:k] - c[0:1]), axis=0, keepdims=True)
    th_ref[...] = tau - s1
    s2_ref[...] = s2
    e2_ref[...] = jnp.exp(s2 - t2[0:1])
    c1_ref[...] = jnp.exp(s1 - t1[0:1]) / zsum


def _peer_route(qt, k1, k2, nt=256):
    m = qt.shape[1]
    nk = PEER_NKEYS
    out = jax.ShapeDtypeStruct((PEER_HEADS, nk, m), F32)
    ospec = pl.BlockSpec((None, nk, nt), lambda i, h: (h, 0, i))
    kspec = pl.BlockSpec((nk, nk), lambda i, h: (0, 0))
    return pl.pallas_call(
        _route_kernel,
        grid=(m // nt, PEER_HEADS),
        in_specs=[pl.BlockSpec((2 * nk, nt), lambda i, h: (h, i)), kspec, kspec],
        out_specs=[ospec] * 4,
        out_shape=[out] * 4,
        compiler_params=_cparams(2, 24 * 2**20),
        name="peer_route",
    )(qt, k1, k2)


PEER_CHUNK = 256
PEER_SUB = 64


def _peer_kernel(ht_ref, th_ref, s2_ref, e2_ref, c1_ref, u_ref, vt_ref, o_ref, act_ref, g_ref, *, te, nt, n_tiles):
    j = pl.program_id(1)
    nk = PEER_NKEYS
    per_chunk = PEER_CHUNK // nk

    @pl.when(j == 0)
    def _():
        o_ref[...] = jnp.zeros_like(o_ref)
        g_ref[1] = jnp.zeros((te, nt), BF16)

    def apply_v_to_previous_tile():
        o_ref[...] += _dot(vt_ref[...], g_ref[(j + 1) % 2])

    @pl.when(j < n_tiles)
    def _():
        cur = j % 2
        for c in range(te // PEER_CHUNK):
            crows = slice(c * PEER_CHUNK, (c + 1) * PEER_CHUNK)
            act_ref[crows, :] = _dot(u_ref[crows, :], ht_ref[...])
            for lb in range(nt // 128):
                lanes = slice(lb * 128, (lb + 1) * 128)
                for sub in range(nk // PEER_SUB):
                    i2 = slice(sub * PEER_SUB, (sub + 1) * PEER_SUB)
                    ws = [jnp.zeros((PEER_SUB, 128), F32) for _ in range(per_chunk)]
                    for h in range(PEER_HEADS):
                        s2t = s2_ref[h, i2, lanes]
                        e2t = e2_ref[h, i2, lanes]
                        for rr in range(per_chunk):
                            r = c * per_chunk + rr
                            th = th_ref[h, r:r + 1, lanes]
                            c1 = c1_ref[h, r:r + 1, lanes]
                            ws[rr] = ws[rr] + jnp.where(s2t >= th, e2t * c1, 0.0)
                    for rr in range(per_chunk):
                        r0 = c * PEER_CHUNK + rr * nk + sub * PEER_SUB
                        rows = slice(r0, r0 + PEER_SUB)
                        g_ref[cur, rows, lanes] = (jax.nn.gelu(act_ref[rows, lanes]) * ws[rr]).astype(BF16)
        apply_v_to_previous_tile()

    @pl.when(j == n_tiles)
    def _():
        apply_v_to_previous_tile()


def _peer(ht, th, s2, e2, c1, u, vt, nt=512, te=1024):
    d, m = ht.shape
    nk = PEER_NKEYS
    groups = te // nk
    n_tiles = N_EXPERTS // te
    assert groups == 8, "first-key rows of a tile must fill one sublane group"
    rspec = pl.BlockSpec((PEER_HEADS, nk, nt), lambda i, j: (0, 0, i))
    gspec = pl.BlockSpec((PEER_HEADS, groups, nt), lambda i, j: (0, jnp.minimum(j, n_tiles - 1), i))
    est = (2 * (d * nt * 2 + 2 * PEER_HEADS * nk * nt * 4 + 2 * te * d * 2 + d * nt * 4)
           + te * nt * 4 + 2 * te * nt * 2 + PEER_CHUNK * nt * 4 + d * nt * 4)
    return pl.pallas_call(
        functools.partial(_peer_kernel, te=te, nt=nt, n_tiles=n_tiles),
        grid=(m // nt, n_tiles + 1),
        in_specs=[
            pl.BlockSpec((d, nt), lambda i, j: (0, i)),
            gspec, rspec, rspec, gspec,
            pl.BlockSpec((te, d), lambda i, j: (jnp.minimum(j, n_tiles - 1), 0)),
            pl.BlockSpec((d, te), lambda i, j: (0, jnp.maximum(j - 1, 0))),
        ],
        out_specs=pl.BlockSpec((d, nt), lambda i, j: (0, i)),
        out_shape=jax.ShapeDtypeStruct((d, m), F32),
        scratch_shapes=[pltpu.VMEM((te, nt), F32), pltpu.VMEM((2, te, nt), BF16)],
        compiler_params=_cparams(2, est + 6 * 2**20),
        name="peer_experts",
    )(ht, th, s2, e2, c1, u, vt)


def kernel(x_prompt, x_sample, cache_k, cache_v, state_gla, c, c_ctx, w_ada, b_ada, w_in, a_ln_g, a_ln_b, a_ws, a_bs, na_rpb, gla_wg_f, gla_bg_f, gla_wg_b, gla_bg_b, gla_norm_g, w_br_a, w_br_b, w_br_c, w_out, ln1_g, ln1_b, ln2_g, ln2_b, peer_wq, peer_k1, peer_k2, peer_u, peer_v):
    n_ctx, n_lat = x_prompt.shape[0], x_sample.shape[0]
    n_layers = w_in.shape[0]
    d = D_MODEL
    mc, ml = n_ctx * SEQ, n_lat * DEC_SEQ
    assert mc % DEC_SEQ == 0, "latent row blocks must start on a DEC_SEQ boundary of the shared token axis"
    assert n_lat + 1 <= 16

    cc = jnp.zeros((16, d), F32).at[0].set(c_ctx).at[1:1 + n_lat].set(c)
    mod = _ada(cc, w_ada, b_ada).reshape(n_layers, 16, N_MOD, 1, d)

    def mod_vec(layer, which):
        return mod[layer, :, which]

    w_main = jnp.concatenate([w_in[:, :, :LR_START], w_in[:, :, LR_END:]], axis=-1).astype(BF16)
    w_lr = jnp.pad(w_in[:, :, LR_START:LR_END], ((0, 0), (0, 0), (0, 128 - (LR_END - LR_START)))).astype(BF16)
    wgf = jnp.pad(gla_wg_f, ((0, 0), (0, 128 - GLA_RANK), (0, 0)))
    wgb = jnp.pad(gla_wg_b, ((0, 0), (GLA_RANK, 128 - 2 * GLA_RANK), (0, 0)))
    wbr_a, wbr_b, wbr_c = w_br_a.astype(BF16), w_br_b.astype(BF16), w_br_c.astype(BF16)
    w_o = w_out.astype(BF16)
    wq_t = jnp.swapaxes(peer_wq, 1, 2).astype(BF16)
    u_b = peer_u.astype(BF16)
    v_t = jnp.swapaxes(peer_v, 1, 2).astype(BF16)
    rope_tabs = _rope_tables(DEC_SEQ)
    bias_tab = _na_bias_tables(na_rpb)

    x = jnp.concatenate([x_prompt.reshape(mc, d), x_sample.reshape(ml, d)], axis=0)
    h = _modulate(x, mod_vec(0, 0), mod_vec(0, 1), mc)

    new_k, new_v, new_s = [], [], []
    for l in range(n_layers):
        z = _matmul(h, w_main[l], BF16, 512, 2048, "in_proj", b_resident=True)
        lr = _matmul(h, w_lr[l], F32, 512, 128, "lr_proj")
        ya = _mixer_a(z, a_ln_g[l], a_ln_b[l], a_ws[l], a_bs[l])
        yb_c, k_l, v_l = _attn_ctx(z, n_ctx)
        yb_l = _attn_lat(z, cache_k, cache_v, bias_tab, l, n_lat, mc // DEC_SEQ)
        gla_args = (wgf[l], wgb[l], gla_bg_f[l], gla_bg_b[l], gla_norm_g[l])
        yc_c, s_l = _gla(z, lr, *gla_args, n_ctx, SEQ, 0, "gla_ctx")
        yc_l, _ = _gla(z, lr, *gla_args, n_lat, DEC_SEQ, mc // DEC_SEQ, "gla_lat",
                       rope_tabs=rope_tabs, s0=state_gla, layer=l)
        new_k.append(k_l)
        new_v.append(v_l)
        new_s.append(s_l)
        yb = jnp.concatenate([yb_c, yb_l], axis=0)
        yc = jnp.concatenate([yc_c, yc_l], axis=0)
        mrg = _merge(ya, yb, yc, wbr_a[l], wbr_b[l], wbr_c[l], z)
        y = _matmul(mrg, w_o[l], F32, 512, 1024, "out_proj")
        x, h2 = _res_ln(x, y, mod_vec(l, 2), ln1_g[l], ln1_b[l], mod_vec(l, 3), mod_vec(l, 4), mc, "res_ln1")
        h2t = h2.T
        qt = _matmul(wq_t[l], h2t, F32, 512, 1024, "peer_q")
        th, s2, e2, c1 = _peer_route(qt, peer_k1[l], peer_k2[l])
        pt = _peer(h2t, th, s2, e2, c1, u_b[l], v_t[l])
        nxt = min(l + 1, n_layers - 1)
        x, h = _res_ln(x, pt.T, mod_vec(l, 5), ln2_g[l], ln2_b[l], mod_vec(nxt, 0), mod_vec(nxt, 1), mc, "res_ln2")

    y_prompt = x[:mc].reshape(n_ctx, SEQ, d)
    y_sample = x[mc:].reshape(n_lat, DEC_SEQ, d)
    return (y_prompt, y_sample, jnp.stack(new_k, axis=1), jnp.stack(new_v, axis=1), jnp.stack(new_s, axis=1))
```

```python
import functools

import numpy as np
import jax
import jax.numpy as jnp
from jax import lax
from jax.experimental import pallas as pl
from jax.experimental.pallas import tpu as pltpu

F32 = jnp.float32
BF16 = jnp.bfloat16

D_MODEL = 2048
SEQ = 256
DEC_SEQ = 2048
GRID_W = 64
CHUNK = 128
A_GROUPS = 8
A_WIDTH = 1024
NA_HEADS = 8
NA_DH = 128
NA_WIN_H = 8
NA_WIN_W = 16
GLA_HEADS = 4
GLA_DK = 128
GLA_DV = 256
GLA_RANK = 16
GLA_GATE_TEMP = 16.0
GLA_CHUNK = 64
ROPE_BASE = 10000.0
PEER_HEADS = 8
PEER_NKEYS = 128
PEER_TOPK = 16
N_EXPERTS = PEER_NKEYS * PEER_NKEYS
N_MOD = 6
ALPHA = 8.0 ** 0.25
EPS = 1e-5
NEG = -1e30

Z_WIDTH = 14336
COL_A = 0
COL_QB, COL_KB, COL_VB = 2048, 3072, 4096
COL_QC, COL_KC, COL_VC, COL_RC = 5120, 5632, 6144, 7168
COL_GA, COL_GB, COL_GC = 8192, 10240, 12288
LR_START, LR_END = 8192, 8224

VMEM_CAP = 56 * 1024 * 1024


def _cparams(n_axes, vmem_bytes, flags=None):
    return pltpu.CompilerParams(
        dimension_semantics=("arbitrary",) * n_axes,
        vmem_limit_bytes=min(int(vmem_bytes), VMEM_CAP),
        flags=flags,
    )


def _mod_row(i, tm, n_ctx_rows):
    start = i * tm
    return jnp.where(start < n_ctx_rows, 0, 1 + (start - n_ctx_rows) // DEC_SEQ)


def _dot(a, b):
    return jnp.dot(a, b, preferred_element_type=F32)


def _dot_nt(a, b):
    return lax.dot_general(a, b, (((1,), (1,)), ((), ())), preferred_element_type=F32)


def _dot_tn(a, b):
    return lax.dot_general(a, b, (((0,), (0,)), ((), ())), preferred_element_type=F32)


def _split_bf16(x):
    hi = x.astype(BF16)
    lo = (x - hi.astype(F32)).astype(BF16)
    return hi, lo


def _ada_kernel(c_ref, w_ref, b_ref, o_ref):
    c = c_ref[...]
    a = (c * jax.nn.sigmoid(c)).astype(BF16)
    o_ref[...] = _dot(a, w_ref[...].astype(BF16)) + b_ref[...]


def _ada(cc, w_ada, b_ada):
    n_layers, d, n = w_ada.shape
    rows = cc.shape[0]
    tn = 1024
    return pl.pallas_call(
        _ada_kernel,
        grid=(n_layers, n // tn),
        in_specs=[
            pl.BlockSpec((rows, d), lambda l, j: (0, 0)),
            pl.BlockSpec((None, d, tn), lambda l, j: (l, 0, j)),
            pl.BlockSpec((None, 1, tn), lambda l, j: (l, 0, j)),
        ],
        out_specs=pl.BlockSpec((None, rows, tn), lambda l, j: (l, 0, j)),
        out_shape=jax.ShapeDtypeStruct((n_layers, rows, n), F32),
        compiler_params=_cparams(2, 2 * d * tn * 4 + 12 * 2**20),
        name="ada_mod",
    )(cc, w_ada, b_ada.reshape(n_layers, 1, n))


def _mm_kernel(a_ref, b_ref, o_ref):
    o_ref[...] = _dot(a_ref[...].astype(BF16), b_ref[...].astype(BF16)).astype(o_ref.dtype)


def _matmul(a, b, out_dtype, tm, tn, name, b_resident=False):
    m, k = a.shape
    _, n = b.shape
    assert m % tm == 0 and n % tn == 0, (m, n, tm, tn)
    est = 2 * (tm * k * a.dtype.itemsize + k * tn * b.dtype.itemsize + tm * tn * 4) + tm * tn * 8
    if b_resident:
        grid = (n // tn, m // tm)
        a_map, b_map, o_map = (lambda j, i: (i, 0)), (lambda j, i: (0, j)), (lambda j, i: (i, j))
    else:
        grid = (m // tm, n // tn)
        a_map, b_map, o_map = (lambda i, j: (i, 0)), (lambda i, j: (0, j)), (lambda i, j: (i, j))
    return pl.pallas_call(
        _mm_kernel,
        grid=grid,
        in_specs=[pl.BlockSpec((tm, k), a_map), pl.BlockSpec((k, tn), b_map)],
        out_specs=pl.BlockSpec((tm, tn), o_map),
        out_shape=jax.ShapeDtypeStruct((m, n), out_dtype),
        compiler_params=_cparams(2, est + 8 * 2**20),
        name=name,
    )(a, b)


def _modulate_kernel(x_ref, sh_ref, sc_ref, h_ref):
    h_ref[...] = (x_ref[...] * (1.0 + sc_ref[...]) + sh_ref[...]).astype(BF16)


def _modulate(x, sh, sc, n_ctx_rows, tm=256):
    m, d = x.shape
    vec = pl.BlockSpec((None, 1, d), lambda i: (_mod_row(i, tm, n_ctx_rows), 0, 0))
    return pl.pallas_call(
        _modulate_kernel,
        grid=(m // tm,),
        in_specs=[pl.BlockSpec((tm, d), lambda i: (i, 0)), vec, vec],
        out_specs=pl.BlockSpec((tm, d), lambda i: (i, 0)),
        out_shape=jax.ShapeDtypeStruct((m, d), BF16),
        compiler_params=_cparams(1, 32 * 2**20),
        name="modulate",
    )(x, sh, sc)


def _res_ln_kernel(x_ref, y_ref, g_ref, lg_ref, lb_ref, sh_ref, sc_ref, xo_ref, h_ref, *, y_transposed, h_transposed):
    y = y_ref[...].astype(F32)
    if y_transposed:
        y = y.T
    t = ALPHA * x_ref[...] + g_ref[...] * y
    mu = jnp.mean(t, axis=-1, keepdims=True)
    tc = t - mu
    var = jnp.mean(tc * tc, axis=-1, keepdims=True)
    xn = tc * lax.rsqrt(var + EPS) * lg_ref[...] + lb_ref[...]
    xo_ref[...] = xn
    h = xn * (1.0 + sc_ref[...]) + sh_ref[...]
    h_ref[...] = (h.T if h_transposed else h).astype(BF16)


def _res_ln(x, y, g, ln_g, ln_b, sh, sc, n_ctx_rows, name, tm=256, y_transposed=False, h_transposed=False):
    m, d = x.shape
    vec = pl.BlockSpec((None, 1, d), lambda i: (_mod_row(i, tm, n_ctx_rows), 0, 0))
    par = pl.BlockSpec((1, d), lambda i: (0, 0))
    row = pl.BlockSpec((tm, d), lambda i: (i, 0))
    col = pl.BlockSpec((d, tm), lambda i: (0, i))
    return pl.pallas_call(
        functools.partial(_res_ln_kernel, y_transposed=y_transposed, h_transposed=h_transposed),
        grid=(m // tm,),
        in_specs=[row, col if y_transposed else row, vec, par, par, vec, vec],
        out_specs=[row, col if h_transposed else row],
        out_shape=[jax.ShapeDtypeStruct((m, d), F32),
                   jax.ShapeDtypeStruct((d, m) if h_transposed else (m, d), BF16)],
        compiler_params=_cparams(1, 40 * 2**20),
        name=name,
    )(x, y, g, ln_g.reshape(1, d), ln_b.reshape(1, d), sh, sc)


def _mixer_a_kernel(a_ref, lg_ref, lb_ref, ws_ref, bs_ref, y_ref, *, n_chunks):
    gw = A_WIDTH // A_GROUPS
    for c in range(n_chunks):
        rows = slice(c * CHUNK, (c + 1) * CHUNK)
        g = jax.nn.gelu(a_ref[rows, :].astype(F32))
        u = g[:, :A_WIDTH]
        v = g[:, A_WIDTH:]
        mu = jnp.mean(v, axis=-1, keepdims=True)
        vc = v - mu
        var = jnp.mean(vc * vc, axis=-1, keepdims=True)
        v = vc * lax.rsqrt(var + EPS) * lg_ref[...] + lb_ref[...]
        for gi in range(A_GROUPS):
            cols = slice(gi * gw, (gi + 1) * gw)
            sp = _dot(ws_ref[gi].astype(BF16), v[:, cols].astype(BF16)) + bs_ref[gi]
            y_ref[rows, cols] = (u[:, cols] * sp).astype(BF16)


def _mixer_a(z, ln_g, ln_b, ws, bs, tm=256):
    m = z.shape[0]
    gw = A_WIDTH // A_GROUPS
    bs_b = jnp.broadcast_to(bs[:, :, None], (A_GROUPS, CHUNK, gw))
    return pl.pallas_call(
        functools.partial(_mixer_a_kernel, n_chunks=tm // CHUNK),
        grid=(m // tm,),
        in_specs=[
            pl.BlockSpec((tm, 2 * A_WIDTH), lambda i: (i, COL_A // (2 * A_WIDTH))),
            pl.BlockSpec((1, A_WIDTH), lambda i: (0, 0)),
            pl.BlockSpec((1, A_WIDTH), lambda i: (0, 0)),
            pl.BlockSpec((A_GROUPS, CHUNK, CHUNK), lambda i: (0, 0, 0)),
            pl.BlockSpec((A_GROUPS, CHUNK, gw), lambda i: (0, 0, 0)),
        ],
        out_specs=pl.BlockSpec((tm, A_WIDTH), lambda i: (i, 0)),
        out_shape=jax.ShapeDtypeStruct((m, A_WIDTH), BF16),
        compiler_params=_cparams(1, 32 * 2**20),
        name="mixer_a",
    )(z, ln_g.reshape(1, A_WIDTH), ln_b.reshape(1, A_WIDTH), ws, bs_b)


def _attn_ctx_kernel(q_ref, k_ref, v_ref, y_ref, ko_ref, vo_ref):
    q, k, v = q_ref[...], k_ref[...], v_ref[...]
    s = _dot_nt(q, k) * (NA_DH ** -0.5)
    m = jnp.max(s, axis=-1, keepdims=True)
    p = jnp.exp(s - m)
    l = jnp.sum(p, axis=-1, keepdims=True)
    o = _dot(p.astype(BF16), v) / l
    y_ref[...] = o.astype(BF16)
    ko_ref[...] = k.astype(F32)
    vo_ref[...] = v.astype(F32)


def _attn_ctx(z, n_ctx):
    qb, kb, vb = COL_QB // NA_DH, COL_KB // NA_DH, COL_VB // NA_DH
    cache = jax.ShapeDtypeStruct((n_ctx, NA_HEADS, SEQ, NA_DH), F32)
    cache_spec = pl.BlockSpec((None, None, SEQ, NA_DH), lambda b, h: (b, h, 0, 0))
    return pl.pallas_call(
        _attn_ctx_kernel,
        grid=(n_ctx, NA_HEADS),
        in_specs=[
            pl.BlockSpec((SEQ, NA_DH), lambda b, h: (b, qb + h)),
            pl.BlockSpec((SEQ, NA_DH), lambda b, h: (b, kb + h)),
            pl.BlockSpec((SEQ, NA_DH), lambda b, h: (b, vb + h)),
        ],
        out_specs=[pl.BlockSpec((SEQ, NA_DH), lambda b, h: (b, h)), cache_spec, cache_spec],
        out_shape=[jax.ShapeDtypeStruct((n_ctx * SEQ, NA_HEADS * NA_DH), BF16), cache, cache],
        compiler_params=_cparams(2, 16 * 2**20),
        name="attn_ctx",
    )(z, z, z)


NA_QROWS = 4
NA_KROWS = 12
NA_NQB = (DEC_SEQ // GRID_W) // NA_QROWS


def _na_block(qb):
    n_rows = DEC_SEQ // GRID_W
    k0 = min(max(NA_QROWS * qb - NA_WIN_H // 2, 0), n_rows - NA_KROWS)
    case = 0 if qb == 0 else (2 if qb == NA_NQB - 1 else 1)
    return k0, case


def _rpb_expand_kernel(rp_ref, oh_ref, neg_ref, o_ref):
    r = rp_ref[...]
    hi = r.astype(BF16)
    r1 = r - hi.astype(F32)
    mid = r1.astype(BF16)
    lo = (r1 - mid.astype(F32)).astype(BF16)
    oh = oh_ref[...]
    o_ref[...] = (_dot(hi, oh) + _dot(mid, oh)) + (_dot(lo, oh) + neg_ref[...])


def _na_bias_tables(rpb):
    n_layers, n_heads, n_dr, n_dc = rpb.shape
    cols = np.arange(GRID_W)
    cs = np.clip(cols - NA_WIN_W // 2, 0, GRID_W - NA_WIN_W)
    ok_c = (cols[None, :] >= cs[:, None]) & (cols[None, :] < cs[:, None] + NA_WIN_W)
    dc = np.clip(cols[None, :] - cols[:, None], -(NA_WIN_W - 1), NA_WIN_W - 1) + NA_WIN_W - 1
    onehot = (np.arange(32)[:, None] == dc.reshape(1, -1)).astype(np.float32)
    negmask = np.where(ok_c.reshape(1, -1), 0.0, NEG).astype(np.float32)
    rows = n_layers * n_heads * n_dr
    rows_pad = -(-rows // 128) * 128
    rp = jnp.pad(rpb.reshape(rows, n_dc), ((0, rows_pad - rows), (0, 32 - n_dc)))
    t = pl.pallas_call(
        _rpb_expand_kernel,
        grid=(rows_pad // 128,),
        in_specs=[pl.BlockSpec((128, 32), lambda i: (i, 0)),
                  pl.BlockSpec((32, GRID_W * GRID_W), lambda i: (0, 0)),
                  pl.BlockSpec((1, GRID_W * GRID_W), lambda i: (0, 0))],
        out_specs=pl.BlockSpec((128, GRID_W * GRID_W), lambda i: (i, 0)),
        out_shape=jax.ShapeDtypeStruct((rows_pad, GRID_W * GRID_W), F32),
        compiler_params=_cparams(1, 16 * 2**20),
        name="rpb_expand",
    )(rp, jnp.asarray(onehot, BF16), jnp.asarray(negmask))
    t = t[:rows].reshape(n_layers, n_heads, n_dr, GRID_W, GRID_W)
    n_rows = DEC_SEQ // GRID_W
    neg_blk = jnp.full((n_layers, n_heads, GRID_W, GRID_W), NEG, F32)
    cases = []
    for qb in (0, 1, NA_NQB - 1):
        k0, _ = _na_block(qb)
        q_rows = []
        for qr in range(NA_QROWS):
            r = NA_QROWS * qb + qr
            rs = min(max(r - NA_WIN_H // 2, 0), n_rows - NA_WIN_H)
            blks = []
            for j in range(NA_KROWS):
                kr = k0 + j
                blks.append(t[:, :, kr - r + NA_WIN_H - 1] if rs <= kr < rs + NA_WIN_H else neg_blk)
            q_rows.append(jnp.concatenate(blks, axis=-1))
        cases.append(jnp.concatenate(q_rows, axis=-2))
    return jnp.stack(cases, axis=2)


def _attn_lat_kernel(q_ref, k_ref, v_ref, ck_ref, cv_ref, bias_ref, y_ref):
    scale = NA_DH ** -0.5
    ck = ck_ref[...].astype(BF16)
    cv = cv_ref[...].astype(BF16)
    nq = NA_QROWS * GRID_W
    nk = NA_KROWS * GRID_W
    for qb in range(NA_NQB):
        k0, case = _na_block(qb)
        q = q_ref[qb * nq:(qb + 1) * nq, :]
        kw = k_ref[k0 * GRID_W:k0 * GRID_W + nk, :]
        vw = v_ref[k0 * GRID_W:k0 * GRID_W + nk, :]
        sw = _dot_nt(q, kw) * scale + bias_ref[case]
        sc = _dot_nt(q, ck) * scale
        m = jnp.maximum(jnp.max(sw, axis=-1, keepdims=True), jnp.max(sc, axis=-1, keepdims=True))
        pw = jnp.exp(sw - m)
        pc = jnp.exp(sc - m)
        l = jnp.sum(pw, axis=-1, keepdims=True) + jnp.sum(pc, axis=-1, keepdims=True)
        o = _dot(pw.astype(BF16), vw) + _dot(pc.astype(BF16), cv)
        y_ref[qb * nq:(qb + 1) * nq, :] = (o / l).astype(BF16)


def _attn_lat(z, cache_k, cache_v, bias_tab, layer, n_lat, row_blk0):
    qb, kb, vb = COL_QB // NA_DH, COL_KB // NA_DH, COL_VB // NA_DH
    past = cache_k.shape[3]
    cspec = pl.BlockSpec((None, None, None, past, NA_DH), lambda b, h: (b, layer, h, 0, 0))
    return pl.pallas_call(
        _attn_lat_kernel,
        grid=(n_lat, NA_HEADS),
        in_specs=[
            pl.BlockSpec((DEC_SEQ, NA_DH), lambda b, h: (row_blk0 + b, qb + h)),
            pl.BlockSpec((DEC_SEQ, NA_DH), lambda b, h: (row_blk0 + b, kb + h)),
            pl.BlockSpec((DEC_SEQ, NA_DH), lambda b, h: (row_blk0 + b, vb + h)),
            cspec, cspec,
            pl.BlockSpec((None, None, 3, NA_QROWS * GRID_W, NA_KROWS * GRID_W), lambda b, h: (layer, h, 0, 0, 0)),
        ],
        out_specs=pl.BlockSpec((DEC_SEQ, NA_DH), lambda b, h: (b, h)),
        out_shape=jax.ShapeDtypeStruct((n_lat * DEC_SEQ, NA_HEADS * NA_DH), BF16),
        compiler_params=_cparams(2, 40 * 2**20),
        name="attn_lat",
    )(z, z, z, cache_k, cache_v, bias_tab)


def _rope_tables(t_len):
    half = GLA_DK // 2
    nf = half // 2
    t = np.arange(t_len)
    inv = ROPE_BASE ** (-np.arange(nf, dtype=np.float32) / nf)
    ang_r = (t // GRID_W).astype(np.float32)[:, None] * inv
    ang_c = (t % GRID_W).astype(np.float32)[:, None] * inv
    cos = np.concatenate([np.cos(ang_r), np.cos(ang_r), np.cos(ang_c), np.cos(ang_c)], axis=-1)
    sin = np.concatenate([-np.sin(ang_r), np.sin(ang_r), -np.sin(ang_c), np.sin(ang_c)], axis=-1)
    return jnp.asarray(cos, F32), jnp.asarray(sin, F32)


def _log_sigmoid(x):
    return jnp.minimum(x, 0.0) - jnp.log(1.0 + jnp.exp(-jnp.abs(x)))


def _gla_kernel(*refs, t_len, rope, has_s0):
    refs = list(refs)
    q_ref, k_ref, v_ref, r_ref, lr_ref, wgf_ref, wgb_ref, bgf_ref, bgb_ref, ng_ref = refs[:10]
    pos = 10
    if rope:
        cos_ref, sin_ref = refs[pos:pos + 2]
        pos += 2
    if has_s0:
        s0_ref = refs[pos]
        pos += 1
    y_ref, sfin_ref, qs, ks, gfs, gbs, of, ob = refs[pos:pos + 8]

    gc = GLA_CHUNK
    n_chunks = t_len // gc
    q = q_ref[...].astype(F32)
    k = k_ref[...].astype(F32)
    if rope:
        lane = lax.broadcasted_iota(jnp.int32, (t_len, GLA_DK), 1)
        first = (lane % (GLA_DK // 2)) < (GLA_DK // 4)

        def swap(x):
            return jnp.where(first, pltpu.roll(x, GLA_DK - GLA_DK // 4, 1), pltpu.roll(x, GLA_DK // 4, 1))

        q = q * cos_ref[...] + swap(q) * sin_ref[...]
        k = k * cos_ref[...] + swap(k) * sin_ref[...]
    qs[...] = q * (GLA_DK ** -0.5)
    ks[...] = k
    lrb = lr_ref[...].astype(BF16)
    inv_temp = 1.0 / GLA_GATE_TEMP
    gfs[...] = _log_sigmoid(_dot(lrb, wgf_ref[...].astype(BF16)) + bgf_ref[...]) * inv_temp
    gbs[...] = _log_sigmoid(_dot(lrb, wgb_ref[...].astype(BF16)) + bgb_ref[...]) * inv_temp

    row = lax.broadcasted_iota(jnp.int32, (gc, gc), 0)
    col = lax.broadcasted_iota(jnp.int32, (gc, gc), 1)
    mid = gc // 2

    def chunk(c, st, g_ref, causal):
        sl = pl.ds(pl.multiple_of(c * gc, gc), gc)
        mask = (row >= col) if causal else (row <= col)
        tri = mask.astype(BF16)
        qc, kc, vc, g = qs[sl, :], ks[sl, :], v_ref[sl, :], g_ref[sl, :]
        ghi, glo = _split_bf16(g)
        b = _dot(tri, ghi) + _dot(tri, glo)
        bmid = b[mid:mid + 1, :]
        tot = b[gc - 1:gc, :] if causal else b[0:1, :]
        q_in = qc * jnp.exp(b)
        qg = qc * jnp.exp(b - bmid)
        kg = kc * jnp.exp(bmid - b)
        kd = kc * jnp.exp(tot - b)
        sc = jnp.where(mask, _dot_nt(qg.astype(BF16), kg.astype(BF16)), 0.0)
        o = _dot(sc.astype(BF16), vc) + _dot_nt(q_in.astype(BF16), st.astype(BF16))
        st_new = jnp.exp(tot) * st + _dot_tn(vc, kd.astype(BF16))
        return o, st_new

    if has_s0:
        st_f0 = s0_ref[0].T
        st_b0 = s0_ref[1].T
    else:
        st_f0 = jnp.zeros((GLA_DV, GLA_DK), F32)
        st_b0 = st_f0

    def scan_step(i, carry):
        st_f, st_b = carry
        cb = n_chunks - 1 - i
        o_f, st_f = chunk(i, st_f, gfs, True)
        o_b, st_b = chunk(cb, st_b, gbs, False)
        of[pl.ds(pl.multiple_of(i * gc, gc), gc), :] = o_f
        ob[pl.ds(pl.multiple_of(cb * gc, gc), gc), :] = o_b
        return st_f, st_b

    st_f, st_b = lax.fori_loop(0, n_chunks, scan_step, (st_f0, st_b0))
    sfin_ref[0] = st_f.T
    sfin_ref[1] = st_b.T

    def finish(c, carry):
        sl = pl.ds(pl.multiple_of(c * gc, gc), gc)
        o = of[sl, :] + ob[sl, :]
        o = o * lax.rsqrt(jnp.mean(o * o, axis=-1, keepdims=True) + EPS) * ng_ref[...]
        r = r_ref[sl, :].astype(F32)
        y_ref[sl, :] = (o * (r * jax.nn.sigmoid(r))).astype(BF16)
        return carry

    lax.fori_loop(0, n_chunks, finish, 0)


def _gla(z, lr, wgf, wgb, bgf, bgb, norm_g, n_batch, t_len, row_blk0, name, rope_tabs=None, s0=None, layer=0):
    qc, kc = COL_QC // GLA_DK, COL_KC // GLA_DK
    vc, rc = COL_VC // GLA_DV, COL_RC // GLA_DV
    kdim = GLA_HEADS * GLA_DK
    in_specs = [
        pl.BlockSpec((t_len, GLA_DK), lambda b, h: (row_blk0 + b, qc + h)),
        pl.BlockSpec((t_len, GLA_DK), lambda b, h: (row_blk0 + b, kc + h)),
        pl.BlockSpec((t_len, GLA_DV), lambda b, h: (row_blk0 + b, vc + h)),
        pl.BlockSpec((t_len, GLA_DV), lambda b, h: (row_blk0 + b, rc + h)),
        pl.BlockSpec((t_len, 128), lambda b, h: (row_blk0 + b, 0)),
        pl.BlockSpec((128, GLA_DK), lambda b, h: (0, h)),
        pl.BlockSpec((128, GLA_DK), lambda b, h: (0, h)),
        pl.BlockSpec((1, GLA_DK), lambda b, h: (0, h)),
        pl.BlockSpec((1, GLA_DK), lambda b, h: (0, h)),
        pl.BlockSpec((1, GLA_DV), lambda b, h: (0, h)),
    ]
    args = [z, z, z, z, lr, wgf, wgb, bgf.reshape(1, kdim), bgb.reshape(1, kdim),
            norm_g.reshape(1, GLA_HEADS * GLA_DV)]
    if rope_tabs is not None:
        in_specs += [pl.BlockSpec((t_len, GLA_DK), lambda b, h: (0, 0))] * 2
        args += list(rope_tabs)
    if s0 is not None:
        in_specs.append(pl.BlockSpec((None, None, 2, None, GLA_DK, GLA_DV), lambda b, h: (b, layer, 0, h, 0, 0)))
        args.append(s0)
    return pl.pallas_call(
        functools.partial(_gla_kernel, t_len=t_len, rope=rope_tabs is not None, has_s0=s0 is not None),
        grid=(n_batch, GLA_HEADS),
        in_specs=in_specs,
        out_specs=[
            pl.BlockSpec((t_len, GLA_DV), lambda b, h: (b, h)),
            pl.BlockSpec((None, 2, None, GLA_DK, GLA_DV), lambda b, h: (b, 0, h, 0, 0)),
        ],
        out_shape=[
            jax.ShapeDtypeStruct((n_batch * t_len, GLA_HEADS * GLA_DV), BF16),
            jax.ShapeDtypeStruct((n_batch, 2, GLA_HEADS, GLA_DK, GLA_DV), F32),
        ],
        scratch_shapes=[pltpu.VMEM((t_len, GLA_DK), F32)] * 4 + [pltpu.VMEM((t_len, GLA_DV), F32)] * 2,
        compiler_params=_cparams(2, 40 * 2**20),
        name=name,
    )(*args)


def _merge_kernel(ya_ref, yb_ref, yc_ref, wa_ref, wb_ref, wc_ref, ga_ref, gb_ref, gc_ref, o_ref):
    m = jax.nn.sigmoid(ga_ref[...].astype(F32)) * _dot(ya_ref[...], wa_ref[...])
    m += jax.nn.sigmoid(gb_ref[...].astype(F32)) * _dot(yb_ref[...], wb_ref[...])
    m += jax.nn.sigmoid(gc_ref[...].astype(F32)) * _dot(yc_ref[...], wc_ref[...])
    o_ref[...] = m.astype(BF16)


def _merge(ya, yb, yc, wa, wb, wc, z, tm=512, tn=1024):
    m, kw = ya.shape
    d = wa.shape[1]
    y_spec = pl.BlockSpec((tm, kw), lambda i, j: (i, 0))
    w_spec = pl.BlockSpec((kw, tn), lambda i, j: (0, j))

    def gate_spec(col):
        return pl.BlockSpec((tm, tn), lambda i, j: (i, col // tn + j))

    return pl.pallas_call(
        _merge_kernel,
        grid=(m // tm, d // tn),
        in_specs=[y_spec, y_spec, y_spec, w_spec, w_spec, w_spec,
                  gate_spec(COL_GA), gate_spec(COL_GB), gate_spec(COL_GC)],
        out_specs=pl.BlockSpec((tm, tn), lambda i, j: (i, j)),
        out_shape=jax.ShapeDtypeStruct((m, d), BF16),
        compiler_params=_cparams(2, 48 * 2**20),
        name="merge",
    )(ya, yb, yc, wa, wb, wc, z, z, z)


def _extract_top(x, count, rows_out, with_rank=False):
    n = x.shape[1]
    rid = lax.broadcasted_iota(jnp.int32, (rows_out, n), 0)
    vals = jnp.full((rows_out, n), NEG, F32)
    rank = jnp.full(x.shape, float(PEER_TOPK), F32)
    for j in range(count):
        m = jnp.max(x, axis=0, keepdims=True)
        vals = jnp.where(rid == j, m, vals)
        hit = x >= m
        if with_rank and j < PEER_TOPK:
            rank = jnp.where(hit, float(j), rank)
        if j + 1 < count:
            x = jnp.where(hit, NEG, x)
    return (vals, rank) if with_rank else vals


def _route_kernel(q_ref, k1_ref, k2_ref, cnt_ref, r2_ref, e2_ref, c1_ref):
    nk = PEER_NKEYS
    k = PEER_TOPK

    def scores(keys, qt):
        kh, kl = _split_bf16(keys)
        qh, ql = _split_bf16(qt)
        return _dot(kh, qh) + (_dot(kh, ql) + _dot(kl, qh))

    s1 = scores(k1_ref[...], q_ref[0:nk, :])
    s2 = scores(k2_ref[...], q_ref[nk:2 * nk, :])
    t1 = _extract_top(s1, k + 1, 24)
    t2, rank2 = _extract_top(s2, k + 1, 24, with_rank=True)
    n = s1.shape[1]
    rid8 = lax.broadcasted_iota(jnp.int32, (8, n), 0)
    extra = jnp.where(rid8 == 0, t1[k:k + 1] + t2[0:1], jnp.where(rid8 == 1, t1[0:1] + t2[k:k + 1], NEG))
    cand = jnp.concatenate(
        [t1[0:1] + t2[0:k]] + [t1[a:a + 1] + t2[0:8] for a in range(1, k)] + [extra], axis=0)
    c = _extract_top(cand, k + 1, 24)
    tau = 0.5 * (c[k - 1:k] + c[k:k + 1])
    zsum = jnp.sum(jnp.exp(c[0:k] - c[0:1]), axis=0, keepdims=True)
    th = tau - s1
    cnt = jnp.zeros_like(th)
    for b in range(k):
        cnt = cnt + jnp.where(th <= t2[b:b + 1], 1.0, 0.0)
    cnt_ref[...] = cnt
    r2_ref[...] = rank2.astype(BF16)
    e2_ref[...] = jnp.exp(s2 - t2[0:1]).astype(BF16)
    c1_ref[...] = jnp.exp(s1 - t1[0:1]) / zsum


def _peer_route(qt, k1, k2, nt=512):
    m = qt.shape[1]
    nk = PEER_NKEYS
    out_f32 = jax.ShapeDtypeStruct((PEER_HEADS, nk, m), F32)
    out_b16 = jax.ShapeDtypeStruct((PEER_HEADS, nk, m), BF16)
    ospec = pl.BlockSpec((None, nk, nt), lambda i, h: (h, 0, i))
    kspec = pl.BlockSpec((nk, nk), lambda i, h: (0, 0))
    return pl.pallas_call(
        _route_kernel,
        grid=(m // nt, PEER_HEADS),
        in_specs=[pl.BlockSpec((2 * nk, nt), lambda i, h: (h, i)), kspec, kspec],
        out_specs=[ospec] * 4,
        out_shape=[out_f32, out_b16, out_b16, out_f32],
        compiler_params=_cparams(2, 24 * 2**20),
        name="peer_route",
    )(qt, k1, k2)


PEER_RBLK = 4


def _peer_kernel(ht_ref, cnt_ref, r2_in_ref, e2_in_ref, c1_ref, u_ref, vt_ref, o_ref,
                 act_ref, g_ref, r2_ref, e2_ref, *, te, nt):
    j = pl.program_id(1)
    nk = PEER_NKEYS
    groups = te // nk

    @pl.when(j == 0)
    def _():
        o_ref[...] = jnp.zeros_like(o_ref)
        r2_ref[...] = r2_in_ref[...]
        e2_ref[...] = e2_in_ref[...]

    act_ref[...] = _dot(u_ref[...], ht_ref[...]).astype(BF16)

    def row_bf16(ref, h, r, lanes):
        row = jnp.broadcast_to(ref[h, r:r + 1, lanes], (16, 128)).astype(BF16)
        return pltpu.repeat(row, nk // 16, axis=0)

    for lb in range(nt // 128):
        lanes = slice(lb * 128, (lb + 1) * 128)
        for rb in range(groups // PEER_RBLK):
            ws = [jnp.zeros((nk, 128), BF16) for _ in range(PEER_RBLK)]
            for h in range(PEER_HEADS):
                r2t = r2_ref[h * nk:(h + 1) * nk, lanes]
                e2t = e2_ref[h * nk:(h + 1) * nk, lanes]
                for rr in range(PEER_RBLK):
                    r = rb * PEER_RBLK + rr
                    hit = r2t < row_bf16(cnt_ref, h, r, lanes)
                    ws[rr] = ws[rr] + jnp.where(hit, e2t * row_bf16(c1_ref, h, r, lanes), 0.0)
            for rr in range(PEER_RBLK):
                rows = slice((rb * PEER_RBLK + rr) * nk, (rb * PEER_RBLK + rr + 1) * nk)
                g_ref[rows, lanes] = jax.nn.gelu(act_ref[rows, lanes]) * ws[rr]

    o_ref[...] += _dot(vt_ref[...], g_ref[...])


def _peer(ht, cnt, r2, e2, c1, u, vt, nt=512, te=1024):
    d, m = ht.shape
    nk = PEER_NKEYS
    groups = te // nk
    assert groups == 8, "first-key rows of a tile must fill one sublane group"
    rspec = pl.BlockSpec((PEER_HEADS * nk, nt), lambda i, j: (0, i))
    gspec = pl.BlockSpec((PEER_HEADS, groups, nt), lambda i, j: (0, j, i))
    est = (2 * (d * nt * 2 + 2 * PEER_HEADS * nk * nt * 2 + 2 * te * d * 2 + d * nt * 4)
           + 2 * te * nt * 2 + te * nt * 4 + d * nt * 4)
    return pl.pallas_call(
        functools.partial(_peer_kernel, te=te, nt=nt),
        grid=(m // nt, N_EXPERTS // te),
        in_specs=[
            pl.BlockSpec((d, nt), lambda i, j: (0, i)),
            gspec, rspec, rspec, gspec,
            pl.BlockSpec((te, d), lambda i, j: (j, 0)),
            pl.BlockSpec((d, te), lambda i, j: (0, j)),
        ],
        out_specs=pl.BlockSpec((d, nt), lambda i, j: (0, i)),
        out_shape=jax.ShapeDtypeStruct((d, m), F32),
        scratch_shapes=[pltpu.VMEM((te, nt), BF16), pltpu.VMEM((te, nt), BF16),
                        pltpu.VMEM((PEER_HEADS * nk, nt), BF16), pltpu.VMEM((PEER_HEADS * nk, nt), BF16)],
        compiler_params=_cparams(2, est + 2 * PEER_HEADS * nk * nt * 2 + 6 * 2**20),
        name="peer_experts",
    )(ht, cnt, r2.reshape(PEER_HEADS * nk, m), e2.reshape(PEER_HEADS * nk, m), c1, u, vt)


def kernel(x_prompt, x_sample, cache_k, cache_v, state_gla, c, c_ctx, w_ada, b_ada, w_in, a_ln_g, a_ln_b, a_ws, a_bs, na_rpb, gla_wg_f, gla_bg_f, gla_wg_b, gla_bg_b, gla_norm_g, w_br_a, w_br_b, w_br_c, w_out, ln1_g, ln1_b, ln2_g, ln2_b, peer_wq, peer_k1, peer_k2, peer_u, peer_v):
    n_ctx, n_lat = x_prompt.shape[0], x_sample.shape[0]
    n_layers = w_in.shape[0]
    d = D_MODEL
    mc, ml = n_ctx * SEQ, n_lat * DEC_SEQ
    assert mc % DEC_SEQ == 0, "latent row blocks must start on a DEC_SEQ boundary of the shared token axis"
    assert n_lat + 1 <= 16

    cc = jnp.zeros((16, d), F32).at[0].set(c_ctx).at[1:1 + n_lat].set(c)
    mod = _ada(cc, w_ada, b_ada).reshape(n_layers, 16, N_MOD, 1, d)

    def mod_vec(layer, which):
        return mod[layer, :, which]

    w_main = jnp.concatenate([w_in[:, :, :LR_START], w_in[:, :, LR_END:]], axis=-1).astype(BF16)
    w_lr = jnp.pad(w_in[:, :, LR_START:LR_END], ((0, 0), (0, 0), (0, 128 - (LR_END - LR_START)))).astype(BF16)
    wgf = jnp.pad(gla_wg_f, ((0, 0), (0, 128 - GLA_RANK), (0, 0)))
    wgb = jnp.pad(gla_wg_b, ((0, 0), (GLA_RANK, 128 - 2 * GLA_RANK), (0, 0)))
    wbr_a, wbr_b, wbr_c = w_br_a.astype(BF16), w_br_b.astype(BF16), w_br_c.astype(BF16)
    w_o = w_out.astype(BF16)
    wq_t = jnp.swapaxes(peer_wq, 1, 2).astype(BF16)
    u_b = peer_u.astype(BF16)
    v_t = jnp.swapaxes(peer_v, 1, 2).astype(BF16)
    rope_tabs = _rope_tables(DEC_SEQ)
    bias_tab = _na_bias_tables(na_rpb)

    x = jnp.concatenate([x_prompt.reshape(mc, d), x_sample.reshape(ml, d)], axis=0)
    h = _modulate(x, mod_vec(0, 0), mod_vec(0, 1), mc)

    new_k, new_v, new_s = [], [], []
    for l in range(n_layers):
        z = _matmul(h, w_main[l], BF16, 512, 2048, "in_proj", b_resident=True)
        lr = _matmul(h, w_lr[l], F32, 512, 128, "lr_proj")
        ya = _mixer_a(z, a_ln_g[l], a_ln_b[l], a_ws[l], a_bs[l])
        yb_c, k_l, v_l = _attn_ctx(z, n_ctx)
        yb_l = _attn_lat(z, cache_k, cache_v, bias_tab, l, n_lat, mc // DEC_SEQ)
        gla_args = (wgf[l], wgb[l], gla_bg_f[l], gla_bg_b[l], gla_norm_g[l])
        yc_c, s_l = _gla(z, lr, *gla_args, n_ctx, SEQ, 0, "gla_ctx")
        yc_l, _ = _gla(z, lr, *gla_args, n_lat, DEC_SEQ, mc // DEC_SEQ, "gla_lat",
                       rope_tabs=rope_tabs, s0=state_gla, layer=l)
        new_k.append(k_l)
        new_v.append(v_l)
        new_s.append(s_l)
        yb = jnp.concatenate([yb_c, yb_l], axis=0)
        yc = jnp.concatenate([yc_c, yc_l], axis=0)
        mrg = _merge(ya, yb, yc, wbr_a[l], wbr_b[l], wbr_c[l], z)
        y = _matmul(mrg, w_o[l], F32, 512, 1024, "out_proj")
        x, h2t = _res_ln(x, y, mod_vec(l, 2), ln1_g[l], ln1_b[l], mod_vec(l, 3), mod_vec(l, 4), mc, "res_ln1",
                         h_transposed=True)
        qt = _matmul(wq_t[l], h2t, F32, 512, 1024, "peer_q")
        cnt, r2, e2, c1 = _peer_route(qt, peer_k1[l], peer_k2[l])
        pt = _peer(h2t, cnt, r2, e2, c1, u_b[l], v_t[l])
        nxt = min(l + 1, n_layers - 1)
        x, h = _res_ln(x, pt, mod_vec(l, 5), ln2_g[l], ln2_b[l], mod_vec(nxt, 0), mod_vec(nxt, 1), mc, "res_ln2",
                       y_transposed=True)

    y_prompt = x[:mc].reshape(n_ctx, SEQ, d)
    y_sample = x[mc:].reshape(n_lat, DEC_SEQ, d)
    return (y_prompt, y_sample, jnp.stack(new_k, axis=1), jnp.stack(new_v, axis=1), jnp.stack(new_s, axis=1))
```

```python
import functools

import numpy as np
import jax
import jax.numpy as jnp
from jax import lax
from jax.experimental import pallas as pl
from jax.experimental.pallas import tpu as pltpu

F32 = jnp.float32
BF16 = jnp.bfloat16

D_MODEL = 2048
SEQ = 256
DEC_SEQ = 2048
GRID_W = 64
CHUNK = 128
A_GROUPS = 8
A_WIDTH = 1024
NA_HEADS = 8
NA_DH = 128
NA_WIN_H = 8
NA_WIN_W = 16
GLA_HEADS = 4
GLA_DK = 128
GLA_DV = 256
GLA_RANK = 16
GLA_GATE_TEMP = 16.0
GLA_CHUNK = 64
ROPE_BASE = 10000.0
PEER_HEADS = 8
PEER_NKEYS = 128
PEER_TOPK = 16
N_EXPERTS = PEER_NKEYS * PEER_NKEYS
N_MOD = 6
ALPHA = 8.0 ** 0.25
EPS = 1e-5
NEG = -1e30

Z_WIDTH = 14336
COL_A = 0
COL_QB, COL_KB, COL_VB = 2048, 3072, 4096
COL_QC, COL_KC, COL_VC, COL_RC = 5120, 5632, 6144, 7168
COL_GA, COL_GB, COL_GC = 8192, 10240, 12288
LR_START, LR_END = 8192, 8224

VMEM_CAP = 56 * 1024 * 1024


def _cparams(n_axes, vmem_bytes, flags=None):
    return pltpu.CompilerParams(
        dimension_semantics=("arbitrary",) * n_axes,
        vmem_limit_bytes=min(int(vmem_bytes), VMEM_CAP),
        flags=flags,
    )


def _mod_row(i, tm, n_ctx_rows):
    start = i * tm
    return jnp.where(start < n_ctx_rows, 0, 1 + (start - n_ctx_rows) // DEC_SEQ)


def _dot(a, b):
    return jnp.dot(a, b, preferred_element_type=F32)


def _dot_nt(a, b):
    return lax.dot_general(a, b, (((1,), (1,)), ((), ())), preferred_element_type=F32)


def _dot_tn(a, b):
    return lax.dot_general(a, b, (((0,), (0,)), ((), ())), preferred_element_type=F32)


def _split_bf16(x):
    hi = x.astype(BF16)
    lo = (x - hi.astype(F32)).astype(BF16)
    return hi, lo


def _ada_kernel(c_ref, w_ref, b_ref, o_ref):
    c = c_ref[...]
    a = (c * jax.nn.sigmoid(c)).astype(BF16)
    o_ref[...] = _dot(a, w_ref[...].astype(BF16)) + b_ref[...]


def _ada(cc, w_ada, b_ada):
    n_layers, d, n = w_ada.shape
    rows = cc.shape[0]
    tn = 1024
    return pl.pallas_call(
        _ada_kernel,
        grid=(n_layers, n // tn),
        in_specs=[
            pl.BlockSpec((rows, d), lambda l, j: (0, 0)),
            pl.BlockSpec((None, d, tn), lambda l, j: (l, 0, j)),
            pl.BlockSpec((None, 1, tn), lambda l, j: (l, 0, j)),
        ],
        out_specs=pl.BlockSpec((None, rows, tn), lambda l, j: (l, 0, j)),
        out_shape=jax.ShapeDtypeStruct((n_layers, rows, n), F32),
        compiler_params=_cparams(2, 2 * d * tn * 4 + 12 * 2**20),
        name="ada_mod",
    )(cc, w_ada, b_ada.reshape(n_layers, 1, n))


def _mm_kernel(a_ref, b_ref, o_ref):
    o_ref[...] = _dot(a_ref[...].astype(BF16), b_ref[...].astype(BF16)).astype(o_ref.dtype)


def _matmul(a, b, out_dtype, tm, tn, name, b_resident=False, layer=None):
    m, k = a.shape[-2:]
    n = b.shape[-1]
    assert m % tm == 0 and n % tn == 0, (m, n, tm, tn)
    est = 2 * (tm * k * a.dtype.itemsize + k * tn * b.dtype.itemsize + tm * tn * 4) + tm * tn * 8
    if b_resident:
        grid = (n // tn, m // tm)
        a_idx, b_idx, o_map = (lambda j, i: (i, 0)), (lambda j, i: (0, j)), (lambda j, i: (i, j))
    else:
        grid = (m // tm, n // tn)
        a_idx, b_idx, o_map = (lambda i, j: (i, 0)), (lambda i, j: (0, j)), (lambda i, j: (i, j))

    def spec(x, block, idx):
        if x.ndim == 2:
            return pl.BlockSpec(block, idx)
        return pl.BlockSpec((None,) + block, lambda *g: (layer,) + idx(*g))

    return pl.pallas_call(
        _mm_kernel,
        grid=grid,
        in_specs=[spec(a, (tm, k), a_idx), spec(b, (k, tn), b_idx)],
        out_specs=pl.BlockSpec((tm, tn), o_map),
        out_shape=jax.ShapeDtypeStruct((m, n), out_dtype),
        compiler_params=_cparams(2, est + 8 * 2**20),
        name=name,
    )(a, b)


def _modulate_kernel(x_ref, sh_ref, sc_ref, h_ref):
    h_ref[...] = (x_ref[...] * (1.0 + sc_ref[...]) + sh_ref[...]).astype(BF16)


def _modulate(x, sh, sc, n_ctx_rows, tm=256):
    m, d = x.shape
    vec = pl.BlockSpec((None, 1, d), lambda i: (_mod_row(i, tm, n_ctx_rows), 0, 0))
    return pl.pallas_call(
        _modulate_kernel,
        grid=(m // tm,),
        in_specs=[pl.BlockSpec((tm, d), lambda i: (i, 0)), vec, vec],
        out_specs=pl.BlockSpec((tm, d), lambda i: (i, 0)),
        out_shape=jax.ShapeDtypeStruct((m, d), BF16),
        compiler_params=_cparams(1, 32 * 2**20),
        name="modulate",
    )(x, sh, sc)


def _res_ln_kernel(x_ref, y_ref, g_ref, lg_ref, lb_ref, sh_ref, sc_ref, xo_ref, h_ref, *, y_transposed, h_transposed):
    y = y_ref[...].astype(F32)
    if y_transposed:
        y = y.T
    t = ALPHA * x_ref[...] + g_ref[...] * y
    mu = jnp.mean(t, axis=-1, keepdims=True)
    tc = t - mu
    var = jnp.mean(tc * tc, axis=-1, keepdims=True)
    xn = tc * lax.rsqrt(var + EPS) * lg_ref[...] + lb_ref[...]
    xo_ref[...] = xn
    h = xn * (1.0 + sc_ref[...]) + sh_ref[...]
    h_ref[...] = (h.T if h_transposed else h).astype(BF16)


def _res_ln(x, y, g, ln_g, ln_b, sh, sc, n_ctx_rows, name, tm=256, y_transposed=False, h_transposed=False):
    m, d = x.shape
    vec = pl.BlockSpec((None, 1, d), lambda i: (_mod_row(i, tm, n_ctx_rows), 0, 0))
    par = pl.BlockSpec((1, d), lambda i: (0, 0))
    row = pl.BlockSpec((tm, d), lambda i: (i, 0))
    col = pl.BlockSpec((d, tm), lambda i: (0, i))
    return pl.pallas_call(
        functools.partial(_res_ln_kernel, y_transposed=y_transposed, h_transposed=h_transposed),
        grid=(m // tm,),
        in_specs=[row, col if y_transposed else row, vec, par, par, vec, vec],
        out_specs=[row, col if h_transposed else row],
        out_shape=[jax.ShapeDtypeStruct((m, d), F32),
                   jax.ShapeDtypeStruct((d, m) if h_transposed else (m, d), BF16)],
        compiler_params=_cparams(1, 40 * 2**20),
        name=name,
    )(x, y, g, ln_g.reshape(1, d), ln_b.reshape(1, d), sh, sc)


def _mixer_a_kernel(a_ref, lg_ref, lb_ref, ws_ref, bs_ref, y_ref, *, n_chunks):
    gw = A_WIDTH // A_GROUPS
    for c in range(n_chunks):
        rows = slice(c * CHUNK, (c + 1) * CHUNK)
        g = jax.nn.gelu(a_ref[rows, :].astype(F32))
        u = g[:, :A_WIDTH]
        v = g[:, A_WIDTH:]
        mu = jnp.mean(v, axis=-1, keepdims=True)
        vc = v - mu
        var = jnp.mean(vc * vc, axis=-1, keepdims=True)
        v = vc * lax.rsqrt(var + EPS) * lg_ref[...] + lb_ref[...]
        for gi in range(A_GROUPS):
            cols = slice(gi * gw, (gi + 1) * gw)
            sp = _dot(ws_ref[gi].astype(BF16), v[:, cols].astype(BF16)) + bs_ref[gi]
            y_ref[rows, cols] = (u[:, cols] * sp).astype(BF16)


def _mixer_a(z, ln_g, ln_b, ws, bs, tm=256):
    m = z.shape[0]
    gw = A_WIDTH // A_GROUPS
    bs_b = jnp.broadcast_to(bs[:, :, None], (A_GROUPS, CHUNK, gw))
    return pl.pallas_call(
        functools.partial(_mixer_a_kernel, n_chunks=tm // CHUNK),
        grid=(m // tm,),
        in_specs=[
            pl.BlockSpec((tm, 2 * A_WIDTH), lambda i: (i, COL_A // (2 * A_WIDTH))),
            pl.BlockSpec((1, A_WIDTH), lambda i: (0, 0)),
            pl.BlockSpec((1, A_WIDTH), lambda i: (0, 0)),
            pl.BlockSpec((A_GROUPS, CHUNK, CHUNK), lambda i: (0, 0, 0)),
            pl.BlockSpec((A_GROUPS, CHUNK, gw), lambda i: (0, 0, 0)),
        ],
        out_specs=pl.BlockSpec((tm, A_WIDTH), lambda i: (i, 0)),
        out_shape=jax.ShapeDtypeStruct((m, A_WIDTH), BF16),
        compiler_params=_cparams(1, 32 * 2**20),
        name="mixer_a",
    )(z, ln_g.reshape(1, A_WIDTH), ln_b.reshape(1, A_WIDTH), ws, bs_b)


def _attn_ctx_kernel(q_ref, k_ref, v_ref, y_ref, ko_ref, vo_ref):
    q, k, v = q_ref[...], k_ref[...], v_ref[...]
    s = _dot_nt(q, k) * (NA_DH ** -0.5)
    m = jnp.max(s, axis=-1, keepdims=True)
    p = jnp.exp(s - m)
    l = jnp.sum(p, axis=-1, keepdims=True)
    o = _dot(p.astype(BF16), v) / l
    y_ref[...] = o.astype(BF16)
    ko_ref[...] = k.astype(F32)
    vo_ref[...] = v.astype(F32)


def _attn_ctx(z, n_ctx):
    qb, kb, vb = COL_QB // NA_DH, COL_KB // NA_DH, COL_VB // NA_DH
    cache = jax.ShapeDtypeStruct((n_ctx, NA_HEADS, SEQ, NA_DH), F32)
    cache_spec = pl.BlockSpec((None, None, SEQ, NA_DH), lambda b, h: (b, h, 0, 0))
    return pl.pallas_call(
        _attn_ctx_kernel,
        grid=(n_ctx, NA_HEADS),
        in_specs=[
            pl.BlockSpec((SEQ, NA_DH), lambda b, h: (b, qb + h)),
            pl.BlockSpec((SEQ, NA_DH), lambda b, h: (b, kb + h)),
            pl.BlockSpec((SEQ, NA_DH), lambda b, h: (b, vb + h)),
        ],
        out_specs=[pl.BlockSpec((SEQ, NA_DH), lambda b, h: (b, h)), cache_spec, cache_spec],
        out_shape=[jax.ShapeDtypeStruct((n_ctx * SEQ, NA_HEADS * NA_DH), BF16), cache, cache],
        compiler_params=_cparams(2, 16 * 2**20),
        name="attn_ctx",
    )(z, z, z)


NA_QROWS = 4
NA_KROWS = 12
NA_NQB = (DEC_SEQ // GRID_W) // NA_QROWS


def _na_block(qb):
    n_rows = DEC_SEQ // GRID_W
    k0 = min(max(NA_QROWS * qb - NA_WIN_H // 2, 0), n_rows - NA_KROWS)
    case = 0 if qb == 0 else (2 if qb == NA_NQB - 1 else 1)
    return k0, case


def _rpb_expand_kernel(rp_ref, oh_ref, neg_ref, o_ref):
    r = rp_ref[...]
    hi = r.astype(BF16)
    r1 = r - hi.astype(F32)
    mid = r1.astype(BF16)
    lo = (r1 - mid.astype(F32)).astype(BF16)
    oh = oh_ref[...]
    o_ref[...] = (_dot(hi, oh) + _dot(mid, oh)) + (_dot(lo, oh) + neg_ref[...])


def _na_bias_tables(rpb):
    n_layers, n_heads, n_dr, n_dc = rpb.shape
    cols = np.arange(GRID_W)
    cs = np.clip(cols - NA_WIN_W // 2, 0, GRID_W - NA_WIN_W)
    ok_c = (cols[None, :] >= cs[:, None]) & (cols[None, :] < cs[:, None] + NA_WIN_W)
    dc = np.clip(cols[None, :] - cols[:, None], -(NA_WIN_W - 1), NA_WIN_W - 1) + NA_WIN_W - 1
    onehot = (np.arange(32)[:, None] == dc.reshape(1, -1)).astype(np.float32)
    negmask = np.where(ok_c.reshape(1, -1), 0.0, NEG).astype(np.float32)
    rows = n_layers * n_heads * n_dr
    rows_pad = -(-rows // 128) * 128
    rp = jnp.pad(rpb.reshape(rows, n_dc), ((0, rows_pad - rows), (0, 32 - n_dc)))
    t = pl.pallas_call(
        _rpb_expand_kernel,
        grid=(rows_pad // 128,),
        in_specs=[pl.BlockSpec((128, 32), lambda i: (i, 0)),
                  pl.BlockSpec((32, GRID_W * GRID_W), lambda i: (0, 0)),
                  pl.BlockSpec((1, GRID_W * GRID_W), lambda i: (0, 0))],
        out_specs=pl.BlockSpec((128, GRID_W * GRID_W), lambda i: (i, 0)),
        out_shape=jax.ShapeDtypeStruct((rows_pad, GRID_W * GRID_W), F32),
        compiler_params=_cparams(1, 16 * 2**20),
        name="rpb_expand",
    )(rp, jnp.asarray(onehot, BF16), jnp.asarray(negmask))
    t = t[:rows].reshape(n_layers, n_heads, n_dr, GRID_W, GRID_W)
    n_rows = DEC_SEQ // GRID_W
    neg_blk = jnp.full((n_layers, n_heads, GRID_W, GRID_W), NEG, F32)
    cases = []
    for qb in (0, 1, NA_NQB - 1):
        k0, _ = _na_block(qb)
        q_rows = []
        for qr in range(NA_QROWS):
            r = NA_QROWS * qb + qr
            rs = min(max(r - NA_WIN_H // 2, 0), n_rows - NA_WIN_H)
            blks = []
            for j in range(NA_KROWS):
                kr = k0 + j
                blks.append(t[:, :, kr - r + NA_WIN_H - 1] if rs <= kr < rs + NA_WIN_H else neg_blk)
            q_rows.append(jnp.concatenate(blks, axis=-1))
        cases.append(jnp.concatenate(q_rows, axis=-2))
    return jnp.stack(cases, axis=2)


def _attn_lat_kernel(q_ref, k_ref, v_ref, ck_ref, cv_ref, bias_ref, y_ref):
    scale = NA_DH ** -0.5
    ck = ck_ref[...].astype(BF16)
    cv = cv_ref[...].astype(BF16)
    nq = NA_QROWS * GRID_W
    nk = NA_KROWS * GRID_W
    for qb in range(NA_NQB):
        k0, case = _na_block(qb)
        q = q_ref[qb * nq:(qb + 1) * nq, :]
        kw = k_ref[k0 * GRID_W:k0 * GRID_W + nk, :]
        vw = v_ref[k0 * GRID_W:k0 * GRID_W + nk, :]
        sw = _dot_nt(q, kw) * scale + bias_ref[case]
        sc = _dot_nt(q, ck) * scale
        m = jnp.maximum(jnp.max(sw, axis=-1, keepdims=True), jnp.max(sc, axis=-1, keepdims=True))
        pw = jnp.exp(sw - m)
        pc = jnp.exp(sc - m)
        l = jnp.sum(pw, axis=-1, keepdims=True) + jnp.sum(pc, axis=-1, keepdims=True)
        o = _dot(pw.astype(BF16), vw) + _dot(pc.astype(BF16), cv)
        y_ref[qb * nq:(qb + 1) * nq, :] = (o / l).astype(BF16)


def _attn_lat(z, cache_k, cache_v, bias_tab, layer, n_lat, row_blk0):
    qb, kb, vb = COL_QB // NA_DH, COL_KB // NA_DH, COL_VB // NA_DH
    past = cache_k.shape[3]
    cspec = pl.BlockSpec((None, None, None, past, NA_DH), lambda b, h: (b, layer, h, 0, 0))
    return pl.pallas_call(
        _attn_lat_kernel,
        grid=(n_lat, NA_HEADS),
        in_specs=[
            pl.BlockSpec((DEC_SEQ, NA_DH), lambda b, h: (row_blk0 + b, qb + h)),
            pl.BlockSpec((DEC_SEQ, NA_DH), lambda b, h: (row_blk0 + b, kb + h)),
            pl.BlockSpec((DEC_SEQ, NA_DH), lambda b, h: (row_blk0 + b, vb + h)),
            cspec, cspec,
            pl.BlockSpec((None, None, 3, NA_QROWS * GRID_W, NA_KROWS * GRID_W), lambda b, h: (layer, h, 0, 0, 0)),
        ],
        out_specs=pl.BlockSpec((DEC_SEQ, NA_DH), lambda b, h: (b, h)),
        out_shape=jax.ShapeDtypeStruct((n_lat * DEC_SEQ, NA_HEADS * NA_DH), BF16),
        compiler_params=_cparams(2, 40 * 2**20),
        name="attn_lat",
    )(z, z, z, cache_k, cache_v, bias_tab)


def _rope_tables(t_len):
    half = GLA_DK // 2
    nf = half // 2
    t = np.arange(t_len)
    inv = ROPE_BASE ** (-np.arange(nf, dtype=np.float32) / nf)
    ang_r = (t // GRID_W).astype(np.float32)[:, None] * inv
    ang_c = (t % GRID_W).astype(np.float32)[:, None] * inv
    cos = np.concatenate([np.cos(ang_r), np.cos(ang_r), np.cos(ang_c), np.cos(ang_c)], axis=-1)
    sin = np.concatenate([-np.sin(ang_r), np.sin(ang_r), -np.sin(ang_c), np.sin(ang_c)], axis=-1)
    return jnp.asarray(cos, F32), jnp.asarray(sin, F32)


def _log_sigmoid(x):
    return jnp.minimum(x, 0.0) - jnp.log(1.0 + jnp.exp(-jnp.abs(x)))


def _gla_kernel(*refs, t_len, rope, has_s0, hp):
    refs = list(refs)
    q_ref, k_ref, v_ref, r_ref, lr_ref, wgf_ref, wgb_ref, bgf_ref, bgb_ref, ng_ref = refs[:10]
    pos = 10
    if rope:
        cos_ref, sin_ref = refs[pos:pos + 2]
        pos += 2
    if has_s0:
        s0_ref = refs[pos]
        pos += 1
    y_ref, sfin_ref, qs, ks, gfs, gbs, of, ob = refs[pos:pos + 8]

    gc = GLA_CHUNK
    n_chunks = t_len // gc
    inv_temp = 1.0 / GLA_GATE_TEMP
    lrb = lr_ref[...].astype(BF16)
    if rope:
        lane = lax.broadcasted_iota(jnp.int32, (t_len, GLA_DK), 1)
        first = (lane % (GLA_DK // 2)) < (GLA_DK // 4)

        def swap(x):
            return jnp.where(first, pltpu.roll(x, GLA_DK - GLA_DK // 4, 1), pltpu.roll(x, GLA_DK // 4, 1))

    for hh in range(hp):
        kcols = slice(hh * GLA_DK, (hh + 1) * GLA_DK)
        q = q_ref[:, kcols].astype(F32)
        k = k_ref[:, kcols].astype(F32)
        if rope:
            q = q * cos_ref[...] + swap(q) * sin_ref[...]
            k = k * cos_ref[...] + swap(k) * sin_ref[...]
        qs[:, kcols] = q * (GLA_DK ** -0.5)
        ks[:, kcols] = k
        gfs[:, kcols] = _log_sigmoid(_dot(lrb, wgf_ref[:, kcols].astype(BF16)) + bgf_ref[:, kcols]) * inv_temp
        gbs[:, kcols] = _log_sigmoid(_dot(lrb, wgb_ref[:, kcols].astype(BF16)) + bgb_ref[:, kcols]) * inv_temp

    row = lax.broadcasted_iota(jnp.int32, (gc, gc), 0)
    col = lax.broadcasted_iota(jnp.int32, (gc, gc), 1)
    mid = gc // 2

    def chunk(c, hh, st, g_ref, causal):
        sl = pl.ds(pl.multiple_of(c * gc, gc), gc)
        kcols = slice(hh * GLA_DK, (hh + 1) * GLA_DK)
        vcols = slice(hh * GLA_DV, (hh + 1) * GLA_DV)
        mask = (row >= col) if causal else (row <= col)
        tri = mask.astype(BF16)
        qc, kc, vc, g = qs[sl, kcols], ks[sl, kcols], v_ref[sl, vcols], g_ref[sl, kcols]
        ghi, glo = _split_bf16(g)
        b = _dot(tri, ghi) + _dot(tri, glo)
        bmid = b[mid:mid + 1, :]
        tot = b[gc - 1:gc, :] if causal else b[0:1, :]
        q_in = qc * jnp.exp(b)
        qg = qc * jnp.exp(b - bmid)
        kg = kc * jnp.exp(bmid - b)
        kd = kc * jnp.exp(tot - b)
        sc = jnp.where(mask, _dot_nt(qg.astype(BF16), kg.astype(BF16)), 0.0)
        o = _dot(sc.astype(BF16), vc) + _dot_nt(q_in.astype(BF16), st.astype(BF16))
        st_new = jnp.exp(tot) * st + _dot_tn(vc, kd.astype(BF16))
        return o, st_new

    if has_s0:
        init = tuple(s0_ref[dr, hh].T for hh in range(hp) for dr in range(2))
    else:
        init = tuple(jnp.zeros((GLA_DV, GLA_DK), F32) for _ in range(2 * hp))

    def scan_step(i, carry):
        cb = n_chunks - 1 - i
        out = []
        for hh in range(hp):
            vcols = slice(hh * GLA_DV, (hh + 1) * GLA_DV)
            o_f, st_f = chunk(i, hh, carry[2 * hh], gfs, True)
            o_b, st_b = chunk(cb, hh, carry[2 * hh + 1], gbs, False)
            of[pl.ds(pl.multiple_of(i * gc, gc), gc), vcols] = o_f
            ob[pl.ds(pl.multiple_of(cb * gc, gc), gc), vcols] = o_b
            out += [st_f, st_b]
        return tuple(out)

    final = lax.fori_loop(0, n_chunks, scan_step, init)
    for hh in range(hp):
        sfin_ref[0, hh] = final[2 * hh].T
        sfin_ref[1, hh] = final[2 * hh + 1].T

    def finish(c, carry):
        sl = pl.ds(pl.multiple_of(c * gc, gc), gc)
        for hh in range(hp):
            vcols = slice(hh * GLA_DV, (hh + 1) * GLA_DV)
            o = of[sl, vcols] + ob[sl, vcols]
            o = o * lax.rsqrt(jnp.mean(o * o, axis=-1, keepdims=True) + EPS) * ng_ref[:, vcols]
            r = r_ref[sl, vcols].astype(F32)
            y_ref[sl, vcols] = (o * (r * jax.nn.sigmoid(r))).astype(BF16)
        return carry

    lax.fori_loop(0, n_chunks, finish, 0)


def _gla(z, lr, wgf, wgb, bgf, bgb, norm_g, n_batch, t_len, row_blk0, name, hp, rope_tabs=None, s0=None, layer=0):
    kw, vw = hp * GLA_DK, hp * GLA_DV
    assert GLA_HEADS % hp == 0 and COL_QC % kw == 0 and COL_KC % kw == 0 and COL_VC % vw == 0 and COL_RC % vw == 0
    qc, kc, vc, rc = COL_QC // kw, COL_KC // kw, COL_VC // vw, COL_RC // vw
    kdim = GLA_HEADS * GLA_DK
    in_specs = [
        pl.BlockSpec((t_len, kw), lambda b, h: (row_blk0 + b, qc + h)),
        pl.BlockSpec((t_len, kw), lambda b, h: (row_blk0 + b, kc + h)),
        pl.BlockSpec((t_len, vw), lambda b, h: (row_blk0 + b, vc + h)),
        pl.BlockSpec((t_len, vw), lambda b, h: (row_blk0 + b, rc + h)),
        pl.BlockSpec((t_len, 128), lambda b, h: (row_blk0 + b, 0)),
        pl.BlockSpec((128, kw), lambda b, h: (0, h)),
        pl.BlockSpec((128, kw), lambda b, h: (0, h)),
        pl.BlockSpec((1, kw), lambda b, h: (0, h)),
        pl.BlockSpec((1, kw), lambda b, h: (0, h)),
        pl.BlockSpec((1, vw), lambda b, h: (0, h)),
    ]
    args = [z, z, z, z, lr, wgf, wgb, bgf.reshape(1, kdim), bgb.reshape(1, kdim),
            norm_g.reshape(1, GLA_HEADS * GLA_DV)]
    if rope_tabs is not None:
        in_specs += [pl.BlockSpec((t_len, GLA_DK), lambda b, h: (0, 0))] * 2
        args += list(rope_tabs)
    if s0 is not None:
        in_specs.append(pl.BlockSpec((None, None, 2, hp, GLA_DK, GLA_DV), lambda b, h: (b, layer, 0, h, 0, 0)))
        args.append(s0)
    return pl.pallas_call(
        functools.partial(_gla_kernel, t_len=t_len, rope=rope_tabs is not None, has_s0=s0 is not None, hp=hp),
        grid=(n_batch, GLA_HEADS // hp),
        in_specs=in_specs,
        out_specs=[
            pl.BlockSpec((t_len, vw), lambda b, h: (b, h)),
            pl.BlockSpec((None, 2, hp, GLA_DK, GLA_DV), lambda b, h: (b, 0, h, 0, 0)),
        ],
        out_shape=[
            jax.ShapeDtypeStruct((n_batch * t_len, GLA_HEADS * GLA_DV), BF16),
            jax.ShapeDtypeStruct((n_batch, 2, GLA_HEADS, GLA_DK, GLA_DV), F32),
        ],
        scratch_shapes=[pltpu.VMEM((t_len, kw), F32)] * 4 + [pltpu.VMEM((t_len, vw), F32)] * 2,
        compiler_params=_cparams(2, 48 * 2**20),
        name=name,
    )(*args)


def _merge_kernel(ya_ref, yb_ref, yc_ref, wa_ref, wb_ref, wc_ref, ga_ref, gb_ref, gc_ref, o_ref):
    m = jax.nn.sigmoid(ga_ref[...].astype(F32)) * _dot(ya_ref[...], wa_ref[...])
    m += jax.nn.sigmoid(gb_ref[...].astype(F32)) * _dot(yb_ref[...], wb_ref[...])
    m += jax.nn.sigmoid(gc_ref[...].astype(F32)) * _dot(yc_ref[...], wc_ref[...])
    o_ref[...] = m.astype(BF16)


def _merge(ya, yb, yc, wa, wb, wc, z, layer, tm=512, tn=1024):
    m, kw = ya.shape
    d = wa.shape[-1]
    y_spec = pl.BlockSpec((tm, kw), lambda i, j: (i, 0))
    w_spec = pl.BlockSpec((None, kw, tn), lambda i, j: (layer, 0, j))

    def gate_spec(col):
        return pl.BlockSpec((tm, tn), lambda i, j: (i, col // tn + j))

    return pl.pallas_call(
        _merge_kernel,
        grid=(m // tm, d // tn),
        in_specs=[y_spec, y_spec, y_spec, w_spec, w_spec, w_spec,
                  gate_spec(COL_GA), gate_spec(COL_GB), gate_spec(COL_GC)],
        out_specs=pl.BlockSpec((tm, tn), lambda i, j: (i, j)),
        out_shape=jax.ShapeDtypeStruct((m, d), BF16),
        compiler_params=_cparams(2, 48 * 2**20),
        name="merge",
    )(ya, yb, yc, wa, wb, wc, z, z, z)


def _extract_top(x, count, rows_out, with_rank=False):
    n = x.shape[1]
    rid = lax.broadcasted_iota(jnp.int32, (rows_out, n), 0)
    vals = jnp.full((rows_out, n), NEG, F32)
    rank = jnp.full(x.shape, float(PEER_TOPK), F32)
    for j in range(count):
        m = jnp.max(x, axis=0, keepdims=True)
        vals = jnp.where(rid == j, m, vals)
        hit = x >= m
        if with_rank and j < PEER_TOPK:
            rank = jnp.where(hit, float(j), rank)
        if j + 1 < count:
            x = jnp.where(hit, NEG, x)
    return (vals, rank) if with_rank else vals


def _route_kernel(q_ref, k1_ref, k2_ref, cnt_ref, r2_ref, e2_ref, c1_ref):
    nk = PEER_NKEYS
    k = PEER_TOPK

    def scores(keys, qt):
        kh, kl = _split_bf16(keys)
        qh, ql = _split_bf16(qt)
        return _dot(kh, qh) + (_dot(kh, ql) + _dot(kl, qh))

    s1 = scores(k1_ref[...], q_ref[0:nk, :])
    s2 = scores(k2_ref[...], q_ref[nk:2 * nk, :])
    t1 = _extract_top(s1, k + 1, 24)
    t2, rank2 = _extract_top(s2, k + 1, 24, with_rank=True)
    n = s1.shape[1]
    rid8 = lax.broadcasted_iota(jnp.int32, (8, n), 0)
    extra = jnp.where(rid8 == 0, t1[k:k + 1] + t2[0:1], jnp.where(rid8 == 1, t1[0:1] + t2[k:k + 1], NEG))
    cand = jnp.concatenate(
        [t1[0:1] + t2[0:k]] + [t1[a:a + 1] + t2[0:8] for a in range(1, k)] + [extra], axis=0)
    c = _extract_top(cand, k + 1, 24)
    tau = 0.5 * (c[k - 1:k] + c[k:k + 1])
    zsum = jnp.sum(jnp.exp(c[0:k] - c[0:1]), axis=0, keepdims=True)
    th = tau - s1
    cnt = jnp.zeros_like(th)
    for b in range(k):
        cnt = cnt + jnp.where(th <= t2[b:b + 1], 1.0, 0.0)
    cnt_ref[...] = cnt
    r2_ref[...] = rank2.astype(BF16)
    e2_ref[...] = jnp.exp(s2 - t2[0:1]).astype(BF16)
    c1_ref[...] = jnp.exp(s1 - t1[0:1]) / zsum


def _peer_route(qt, k1, k2, nt=512):
    m = qt.shape[1]
    nk = PEER_NKEYS
    out_f32 = jax.ShapeDtypeStruct((PEER_HEADS, nk, m), F32)
    out_b16 = jax.ShapeDtypeStruct((PEER_HEADS, nk, m), BF16)
    ospec = pl.BlockSpec((None, nk, nt), lambda i, h: (h, 0, i))
    kspec = pl.BlockSpec((nk, nk), lambda i, h: (0, 0))
    return pl.pallas_call(
        _route_kernel,
        grid=(m // nt, PEER_HEADS),
        in_specs=[pl.BlockSpec((2 * nk, nt), lambda i, h: (h, i)), kspec, kspec],
        out_specs=[ospec] * 4,
        out_shape=[out_f32, out_b16, out_b16, out_f32],
        compiler_params=_cparams(2, 24 * 2**20),
        name="peer_route",
    )(qt, k1, k2)


PEER_TE = 1024
PEER_RBLK = 4


def _peer_kernel(ht_ref, cnt_ref, r2_in_ref, e2_in_ref, c1_ref, u_ref, vt_ref, o_ref,
                 act_ref, g_ref, r2_ref, e2_ref, *, te, nt):
    j = pl.program_id(1)
    nk = PEER_NKEYS
    groups = te // nk

    @pl.when(j == 0)
    def _():
        o_ref[...] = jnp.zeros_like(o_ref)
        r2_ref[...] = r2_in_ref[...]
        e2_ref[...] = e2_in_ref[...]

    act_ref[...] = _dot(u_ref[...], ht_ref[...]).astype(BF16)

    def row_bf16(ref, h, r, lanes):
        row = jnp.broadcast_to(ref[h, r:r + 1, lanes], (16, 128)).astype(BF16)
        return pltpu.repeat(row, nk // 16, axis=0)

    for lb in range(nt // 128):
        lanes = slice(lb * 128, (lb + 1) * 128)
        for rb in range(groups // PEER_RBLK):
            ws = [jnp.zeros((nk, 128), BF16) for _ in range(PEER_RBLK)]
            for h in range(PEER_HEADS):
                r2t = r2_ref[h * nk:(h + 1) * nk, lanes]
                e2t = e2_ref[h * nk:(h + 1) * nk, lanes]
                for rr in range(PEER_RBLK):
                    r = rb * PEER_RBLK + rr
                    hit = r2t < row_bf16(cnt_ref, h, r, lanes)
                    ws[rr] = ws[rr] + jnp.where(hit, e2t * row_bf16(c1_ref, h, r, lanes), 0.0)
            for rr in range(PEER_RBLK):
                rows = slice((rb * PEER_RBLK + rr) * nk, (rb * PEER_RBLK + rr + 1) * nk)
                g_ref[rows, lanes] = jax.nn.gelu(act_ref[rows, lanes]) * ws[rr]

    o_ref[...] += _dot(vt_ref[...], g_ref[...])


def _peer(ht, cnt, r2, e2, c1, u, vt, layer, nt=512):
    d, m = ht.shape
    te = PEER_TE
    nk = PEER_NKEYS
    groups = te // nk
    assert groups == 8, "first-key rows of a tile must fill one sublane group"
    rspec = pl.BlockSpec((PEER_HEADS * nk, nt), lambda i, j: (0, i))
    gspec = pl.BlockSpec((PEER_HEADS, groups, nt), lambda i, j: (0, j, i))
    est = (2 * (d * nt * 2 + 2 * PEER_HEADS * nk * nt * 2 + 2 * te * d * 2 + d * nt * 4)
           + 2 * te * nt * 2 + te * nt * 4 + d * nt * 4)
    return pl.pallas_call(
        functools.partial(_peer_kernel, te=te, nt=nt),
        grid=(m // nt, N_EXPERTS // te),
        in_specs=[
            pl.BlockSpec((d, nt), lambda i, j: (0, i)),
            gspec, rspec, rspec, gspec,
            pl.BlockSpec((None, te, d), lambda i, j: (layer, j, 0)),
            pl.BlockSpec((None, None, d, te), lambda i, j: (layer, j, 0, 0)),
        ],
        out_specs=pl.BlockSpec((d, nt), lambda i, j: (0, i)),
        out_shape=jax.ShapeDtypeStruct((d, m), F32),
        scratch_shapes=[pltpu.VMEM((te, nt), BF16), pltpu.VMEM((te, nt), BF16),
                        pltpu.VMEM((PEER_HEADS * nk, nt), BF16), pltpu.VMEM((PEER_HEADS * nk, nt), BF16)],
        compiler_params=_cparams(2, est + 2 * PEER_HEADS * nk * nt * 2 + 6 * 2**20),
        name="peer_experts",
    )(ht, cnt, r2.reshape(PEER_HEADS * nk, m), e2.reshape(PEER_HEADS * nk, m), c1, u, vt)


def kernel(x_prompt, x_sample, cache_k, cache_v, state_gla, c, c_ctx, w_ada, b_ada, w_in, a_ln_g, a_ln_b, a_ws, a_bs, na_rpb, gla_wg_f, gla_bg_f, gla_wg_b, gla_bg_b, gla_norm_g, w_br_a, w_br_b, w_br_c, w_out, ln1_g, ln1_b, ln2_g, ln2_b, peer_wq, peer_k1, peer_k2, peer_u, peer_v):
    n_ctx, n_lat = x_prompt.shape[0], x_sample.shape[0]
    n_layers = w_in.shape[0]
    d = D_MODEL
    mc, ml = n_ctx * SEQ, n_lat * DEC_SEQ
    assert mc % DEC_SEQ == 0, "latent row blocks must start on a DEC_SEQ boundary of the shared token axis"
    assert n_lat + 1 <= 16

    cc = jnp.zeros((16, d), F32).at[0].set(c_ctx).at[1:1 + n_lat].set(c)
    mod = _ada(cc, w_ada, b_ada).reshape(n_layers, 16, N_MOD, 1, d)

    def mod_vec(layer, which):
        return mod[layer, :, which]

    w_main = jnp.concatenate([w_in[:, :, :LR_START], w_in[:, :, LR_END:]], axis=-1).astype(BF16)
    w_lr = jnp.pad(w_in[:, :, LR_START:LR_END], ((0, 0), (0, 0), (0, 128 - (LR_END - LR_START)))).astype(BF16)
    wgf = jnp.pad(gla_wg_f, ((0, 0), (0, 128 - GLA_RANK), (0, 0)))
    wgb = jnp.pad(gla_wg_b, ((0, 0), (GLA_RANK, 128 - 2 * GLA_RANK), (0, 0)))
    wbr_a, wbr_b, wbr_c = w_br_a.astype(BF16), w_br_b.astype(BF16), w_br_c.astype(BF16)
    w_o = w_out.astype(BF16)
    wq_t = jnp.swapaxes(peer_wq, 1, 2).astype(BF16)
    u_b = peer_u.astype(BF16)
    v_t = jnp.swapaxes(peer_v.reshape(n_layers, N_EXPERTS // PEER_TE, PEER_TE, d), 2, 3).astype(BF16)
    rope_tabs = _rope_tables(DEC_SEQ)
    bias_tab = _na_bias_tables(na_rpb)

    x = jnp.concatenate([x_prompt.reshape(mc, d), x_sample.reshape(ml, d)], axis=0)
    h = _modulate(x, mod_vec(0, 0), mod_vec(0, 1), mc)

    new_k, new_v, new_s = [], [], []
    for l in range(n_layers):
        z = _matmul(h, w_main, BF16, 512, 2048, "in_proj", b_resident=True, layer=l)
        lr = _matmul(h, w_lr[l], F32, 512, 128, "lr_proj")
        ya = _mixer_a(z, a_ln_g[l], a_ln_b[l], a_ws[l], a_bs[l])
        yb_c, k_l, v_l = _attn_ctx(z, n_ctx)
        yb_l = _attn_lat(z, cache_k, cache_v, bias_tab, l, n_lat, mc // DEC_SEQ)
        gla_args = (wgf[l], wgb[l], gla_bg_f[l], gla_bg_b[l], gla_norm_g[l])
        yc_c, s_l = _gla(z, lr, *gla_args, n_ctx, SEQ, 0, "gla_ctx", 4)
        yc_l, _ = _gla(z, lr, *gla_args, n_lat, DEC_SEQ, mc // DEC_SEQ, "gla_lat", 2,
                       rope_tabs=rope_tabs, s0=state_gla, layer=l)
        new_k.append(k_l)
        new_v.append(v_l)
        new_s.append(s_l)
        yb = jnp.concatenate([yb_c, yb_l], axis=0)
        yc = jnp.concatenate([yc_c, yc_l], axis=0)
        mrg = _merge(ya, yb, yc, wbr_a, wbr_b, wbr_c, z, l)
        y = _matmul(mrg, w_o, F32, 512, 1024, "out_proj", layer=l)
        x, h2t = _res_ln(x, y, mod_vec(l, 2), ln1_g[l], ln1_b[l], mod_vec(l, 3), mod_vec(l, 4), mc, "res_ln1",
                         h_transposed=True)
        qt = _matmul(wq_t, h2t, F32, 512, 1024, "peer_q", layer=l)
        cnt, r2, e2, c1 = _peer_route(qt, peer_k1[l], peer_k2[l])
        pt = _peer(h2t, cnt, r2, e2, c1, u_b, v_t, l)
        nxt = min(l + 1, n_layers - 1)
        x, h = _res_ln(x, pt, mod_vec(l, 5), ln2_g[l], ln2_b[l], mod_vec(nxt, 0), mod_vec(nxt, 1), mc, "res_ln2",
                       y_transposed=True)

    y_prompt = x[:mc].reshape(n_ctx, SEQ, d)
    y_sample = x[mc:].reshape(n_lat, DEC_SEQ, d)
    return (y_prompt, y_sample, jnp.stack(new_k, axis=1), jnp.stack(new_v, axis=1), jnp.stack(new_s, axis=1))
```

```python
import functools

import numpy as np
import jax
import jax.numpy as jnp
from jax import lax
from jax.experimental import pallas as pl
from jax.experimental.pallas import tpu as pltpu

F32 = jnp.float32
BF16 = jnp.bfloat16

D_MODEL = 2048
SEQ = 256
DEC_SEQ = 2048
GRID_W = 64
CHUNK = 128
A_GROUPS = 8
A_WIDTH = 1024
NA_HEADS = 8
NA_DH = 128
NA_WIN_H = 8
NA_WIN_W = 16
GLA_HEADS = 4
GLA_DK = 128
GLA_DV = 256
GLA_RANK = 16
GLA_GATE_TEMP = 16.0
GLA_CHUNK = 64
ROPE_BASE = 10000.0
PEER_HEADS = 8
PEER_NKEYS = 128
PEER_TOPK = 16
N_EXPERTS = PEER_NKEYS * PEER_NKEYS
N_MOD = 6
ALPHA = 8.0 ** 0.25
EPS = 1e-5
NEG = -1e30

Z_WIDTH = 14336
COL_A = 0
COL_QB, COL_KB, COL_VB = 2048, 3072, 4096
COL_QC, COL_KC, COL_VC, COL_RC = 5120, 5632, 6144, 7168
COL_GA, COL_GB, COL_GC = 8192, 10240, 12288
LR_START, LR_END = 8192, 8224

VMEM_CAP = 56 * 1024 * 1024


def _cparams(n_axes, vmem_bytes, flags=None):
    return pltpu.CompilerParams(
        dimension_semantics=("arbitrary",) * n_axes,
        vmem_limit_bytes=min(int(vmem_bytes), VMEM_CAP),
        flags=flags,
    )


def _mod_row(i, tm, n_ctx_rows):
    start = i * tm
    return jnp.where(start < n_ctx_rows, 0, 1 + (start - n_ctx_rows) // DEC_SEQ)


def _dot(a, b):
    return jnp.dot(a, b, preferred_element_type=F32)


def _dot_nt(a, b):
    return lax.dot_general(a, b, (((1,), (1,)), ((), ())), preferred_element_type=F32)


def _dot_tn(a, b):
    return lax.dot_general(a, b, (((0,), (0,)), ((), ())), preferred_element_type=F32)


def _split_bf16(x):
    hi = x.astype(BF16)
    lo = (x - hi.astype(F32)).astype(BF16)
    return hi, lo


def _ada_kernel(c_ref, w_ref, b_ref, o_ref):
    c = c_ref[...]
    a = (c * jax.nn.sigmoid(c)).astype(BF16)
    o_ref[...] = _dot(a, w_ref[...].astype(BF16)) + b_ref[...]


def _ada(cc, w_ada, b_ada):
    n_layers, d, n = w_ada.shape
    rows = cc.shape[0]
    tn = 1024
    return pl.pallas_call(
        _ada_kernel,
        grid=(n_layers, n // tn),
        in_specs=[
            pl.BlockSpec((rows, d), lambda l, j: (0, 0)),
            pl.BlockSpec((None, d, tn), lambda l, j: (l, 0, j)),
            pl.BlockSpec((None, 1, tn), lambda l, j: (l, 0, j)),
        ],
        out_specs=pl.BlockSpec((None, rows, tn), lambda l, j: (l, 0, j)),
        out_shape=jax.ShapeDtypeStruct((n_layers, rows, n), F32),
        compiler_params=_cparams(2, 2 * d * tn * 4 + 12 * 2**20),
        name="ada_mod",
    )(cc, w_ada, b_ada.reshape(n_layers, 1, n))


def _mm_kernel(a_ref, b_ref, o_ref):
    o_ref[...] = _dot(a_ref[...].astype(BF16), b_ref[...].astype(BF16)).astype(o_ref.dtype)


def _matmul(a, b, out_dtype, tm, tn, name, b_resident=False, layer=None):
    m, k = a.shape[-2:]
    n = b.shape[-1]
    assert m % tm == 0 and n % tn == 0, (m, n, tm, tn)
    est = 2 * (tm * k * a.dtype.itemsize + k * tn * b.dtype.itemsize + tm * tn * 4) + tm * tn * 8
    if b_resident:
        grid = (n // tn, m // tm)
        a_idx, b_idx, o_map = (lambda j, i: (i, 0)), (lambda j, i: (0, j)), (lambda j, i: (i, j))
    else:
        grid = (m // tm, n // tn)
        a_idx, b_idx, o_map = (lambda i, j: (i, 0)), (lambda i, j: (0, j)), (lambda i, j: (i, j))

    def spec(x, block, idx):
        if x.ndim == 2:
            return pl.BlockSpec(block, idx)
        return pl.BlockSpec((None,) + block, lambda *g: (layer,) + idx(*g))

    return pl.pallas_call(
        _mm_kernel,
        grid=grid,
        in_specs=[spec(a, (tm, k), a_idx), spec(b, (k, tn), b_idx)],
        out_specs=pl.BlockSpec((tm, tn), o_map),
        out_shape=jax.ShapeDtypeStruct((m, n), out_dtype),
        compiler_params=_cparams(2, est + 8 * 2**20),
        name=name,
    )(a, b)


def _modulate_kernel(x_ref, sh_ref, sc_ref, h_ref):
    h_ref[...] = (x_ref[...] * (1.0 + sc_ref[...]) + sh_ref[...]).astype(BF16)


def _modulate(x, sh, sc, n_ctx_rows, tm=256):
    m, d = x.shape
    vec = pl.BlockSpec((None, 1, d), lambda i: (_mod_row(i, tm, n_ctx_rows), 0, 0))
    return pl.pallas_call(
        _modulate_kernel,
        grid=(m // tm,),
        in_specs=[pl.BlockSpec((tm, d), lambda i: (i, 0)), vec, vec],
        out_specs=pl.BlockSpec((tm, d), lambda i: (i, 0)),
        out_shape=jax.ShapeDtypeStruct((m, d), BF16),
        compiler_params=_cparams(1, 32 * 2**20),
        name="modulate",
    )(x, sh, sc)


def _res_ln_kernel(x_ref, y_ref, g_ref, lg_ref, lb_ref, sh_ref, sc_ref, xo_ref, h_ref, *, y_transposed, h_transposed):
    y = y_ref[...].astype(F32)
    if y_transposed:
        y = y.T
    t = ALPHA * x_ref[...] + g_ref[...] * y
    mu = jnp.mean(t, axis=-1, keepdims=True)
    tc = t - mu
    var = jnp.mean(tc * tc, axis=-1, keepdims=True)
    xn = tc * lax.rsqrt(var + EPS) * lg_ref[...] + lb_ref[...]
    xo_ref[...] = xn
    h = xn * (1.0 + sc_ref[...]) + sh_ref[...]
    h_ref[...] = (h.T if h_transposed else h).astype(BF16)


def _res_ln(x, y, g, ln_g, ln_b, sh, sc, n_ctx_rows, name, tm=256, y_transposed=False, h_transposed=False):
    m, d = x.shape
    vec = pl.BlockSpec((None, 1, d), lambda i: (_mod_row(i, tm, n_ctx_rows), 0, 0))
    par = pl.BlockSpec((1, d), lambda i: (0, 0))
    row = pl.BlockSpec((tm, d), lambda i: (i, 0))
    col = pl.BlockSpec((d, tm), lambda i: (0, i))
    return pl.pallas_call(
        functools.partial(_res_ln_kernel, y_transposed=y_transposed, h_transposed=h_transposed),
        grid=(m // tm,),
        in_specs=[row, col if y_transposed else row, vec, par, par, vec, vec],
        out_specs=[row, col if h_transposed else row],
        out_shape=[jax.ShapeDtypeStruct((m, d), F32),
                   jax.ShapeDtypeStruct((d, m) if h_transposed else (m, d), BF16)],
        compiler_params=_cparams(1, 40 * 2**20),
        name=name,
    )(x, y, g, ln_g.reshape(1, d), ln_b.reshape(1, d), sh, sc)


def _mixer_a_kernel(a_ref, lg_ref, lb_ref, ws_ref, bs_ref, y_ref, *, n_chunks):
    gw = A_WIDTH // A_GROUPS
    for c in range(n_chunks):
        rows = slice(c * CHUNK, (c + 1) * CHUNK)
        g = jax.nn.gelu(a_ref[rows, :].astype(F32))
        u = g[:, :A_WIDTH]
        v = g[:, A_WIDTH:]
        mu = jnp.mean(v, axis=-1, keepdims=True)
        vc = v - mu
        var = jnp.mean(vc * vc, axis=-1, keepdims=True)
        v = vc * lax.rsqrt(var + EPS) * lg_ref[...] + lb_ref[...]
        for gi in range(A_GROUPS):
            cols = slice(gi * gw, (gi + 1) * gw)
            sp = _dot(ws_ref[gi].astype(BF16), v[:, cols].astype(BF16)) + bs_ref[gi]
            y_ref[rows, cols] = (u[:, cols] * sp).astype(BF16)


def _mixer_a(z, ln_g, ln_b, ws, bs, tm=256):
    m = z.shape[0]
    gw = A_WIDTH // A_GROUPS
    bs_b = jnp.broadcast_to(bs[:, :, None], (A_GROUPS, CHUNK, gw))
    return pl.pallas_call(
        functools.partial(_mixer_a_kernel, n_chunks=tm // CHUNK),
        grid=(m // tm,),
        in_specs=[
            pl.BlockSpec((tm, 2 * A_WIDTH), lambda i: (i, COL_A // (2 * A_WIDTH))),
            pl.BlockSpec((1, A_WIDTH), lambda i: (0, 0)),
            pl.BlockSpec((1, A_WIDTH), lambda i: (0, 0)),
            pl.BlockSpec((A_GROUPS, CHUNK, CHUNK), lambda i: (0, 0, 0)),
            pl.BlockSpec((A_GROUPS, CHUNK, gw), lambda i: (0, 0, 0)),
        ],
        out_specs=pl.BlockSpec((tm, A_WIDTH), lambda i: (i, 0)),
        out_shape=jax.ShapeDtypeStruct((m, A_WIDTH), BF16),
        compiler_params=_cparams(1, 32 * 2**20),
        name="mixer_a",
    )(z, ln_g.reshape(1, A_WIDTH), ln_b.reshape(1, A_WIDTH), ws, bs_b)


def _attn_ctx_kernel(q_ref, k_ref, v_ref, y_ref, ko_ref, vo_ref):
    for h in range(NA_HEADS):
        cols = slice(h * NA_DH, (h + 1) * NA_DH)
        q, k, v = q_ref[:, cols], k_ref[:, cols], v_ref[:, cols]
        s = _dot_nt(q, k) * (NA_DH ** -0.5)
        m = jnp.max(s, axis=-1, keepdims=True)
        p = jnp.exp(s - m)
        l = jnp.sum(p, axis=-1, keepdims=True)
        o = _dot(p.astype(BF16), v) / l
        y_ref[:, cols] = o.astype(BF16)
        ko_ref[h] = k.astype(F32)
        vo_ref[h] = v.astype(F32)


def _attn_ctx(z, n_ctx):
    width = NA_HEADS * NA_DH
    assert COL_QB % width == 0 and COL_KB % width == 0 and COL_VB % width == 0
    cache = jax.ShapeDtypeStruct((n_ctx, NA_HEADS, SEQ, NA_DH), F32)
    cache_spec = pl.BlockSpec((None, NA_HEADS, SEQ, NA_DH), lambda b: (b, 0, 0, 0))
    return pl.pallas_call(
        _attn_ctx_kernel,
        grid=(n_ctx,),
        in_specs=[
            pl.BlockSpec((SEQ, width), lambda b: (b, COL_QB // width)),
            pl.BlockSpec((SEQ, width), lambda b: (b, COL_KB // width)),
            pl.BlockSpec((SEQ, width), lambda b: (b, COL_VB // width)),
        ],
        out_specs=[pl.BlockSpec((SEQ, width), lambda b: (b, 0)), cache_spec, cache_spec],
        out_shape=[jax.ShapeDtypeStruct((n_ctx * SEQ, width), BF16), cache, cache],
        compiler_params=_cparams(1, 24 * 2**20),
        name="attn_ctx",
    )(z, z, z)


NA_QROWS = 4
NA_KROWS = 12
NA_NQB = (DEC_SEQ // GRID_W) // NA_QROWS


def _na_block(qb):
    n_rows = DEC_SEQ // GRID_W
    k0 = min(max(NA_QROWS * qb - NA_WIN_H // 2, 0), n_rows - NA_KROWS)
    case = 0 if qb == 0 else (2 if qb == NA_NQB - 1 else 1)
    return k0, case


def _rpb_expand_kernel(rp_ref, oh_ref, neg_ref, o_ref):
    r = rp_ref[...]
    hi = r.astype(BF16)
    r1 = r - hi.astype(F32)
    mid = r1.astype(BF16)
    lo = (r1 - mid.astype(F32)).astype(BF16)
    oh = oh_ref[...]
    o_ref[...] = (_dot(hi, oh) + _dot(mid, oh)) + (_dot(lo, oh) + neg_ref[...])


def _na_bias_tables(rpb):
    n_layers, n_heads, n_dr, n_dc = rpb.shape
    cols = np.arange(GRID_W)
    cs = np.clip(cols - NA_WIN_W // 2, 0, GRID_W - NA_WIN_W)
    ok_c = (cols[None, :] >= cs[:, None]) & (cols[None, :] < cs[:, None] + NA_WIN_W)
    dc = np.clip(cols[None, :] - cols[:, None], -(NA_WIN_W - 1), NA_WIN_W - 1) + NA_WIN_W - 1
    onehot = (np.arange(32)[:, None] == dc.reshape(1, -1)).astype(np.float32)
    negmask = np.where(ok_c.reshape(1, -1), 0.0, NEG).astype(np.float32)
    rows = n_layers * n_heads * n_dr
    rows_pad = -(-rows // 128) * 128
    rp = jnp.pad(rpb.reshape(rows, n_dc), ((0, rows_pad - rows), (0, 32 - n_dc)))
    t = pl.pallas_call(
        _rpb_expand_kernel,
        grid=(rows_pad // 128,),
        in_specs=[pl.BlockSpec((128, 32), lambda i: (i, 0)),
                  pl.BlockSpec((32, GRID_W * GRID_W), lambda i: (0, 0)),
                  pl.BlockSpec((1, GRID_W * GRID_W), lambda i: (0, 0))],
        out_specs=pl.BlockSpec((128, GRID_W * GRID_W), lambda i: (i, 0)),
        out_shape=jax.ShapeDtypeStruct((rows_pad, GRID_W * GRID_W), F32),
        compiler_params=_cparams(1, 16 * 2**20),
        name="rpb_expand",
    )(rp, jnp.asarray(onehot, BF16), jnp.asarray(negmask))
    t = t[:rows].reshape(n_layers, n_heads, n_dr, GRID_W, GRID_W)
    n_rows = DEC_SEQ // GRID_W
    neg_blk = jnp.full((n_layers, n_heads, GRID_W, GRID_W), NEG, F32)
    cases = []
    for qb in (0, 1, NA_NQB - 1):
        k0, _ = _na_block(qb)
        q_rows = []
        for qr in range(NA_QROWS):
            r = NA_QROWS * qb + qr
            rs = min(max(r - NA_WIN_H // 2, 0), n_rows - NA_WIN_H)
            blks = []
            for j in range(NA_KROWS):
                kr = k0 + j
                blks.append(t[:, :, kr - r + NA_WIN_H - 1] if rs <= kr < rs + NA_WIN_H else neg_blk)
            q_rows.append(jnp.concatenate(blks, axis=-1))
        cases.append(jnp.concatenate(q_rows, axis=-2))
    return jnp.stack(cases, axis=2)


def _attn_lat_kernel(q_ref, k_ref, v_ref, ck_ref, cv_ref, bias_ref, y_ref):
    scale = NA_DH ** -0.5
    ck = ck_ref[...].astype(BF16)
    cv = cv_ref[...].astype(BF16)
    nq = NA_QROWS * GRID_W
    nk = NA_KROWS * GRID_W
    for qb in range(NA_NQB):
        k0, case = _na_block(qb)
        q = q_ref[qb * nq:(qb + 1) * nq, :]
        kw = k_ref[k0 * GRID_W:k0 * GRID_W + nk, :]
        vw = v_ref[k0 * GRID_W:k0 * GRID_W + nk, :]
        sw = _dot_nt(q, kw) * scale + bias_ref[case]
        sc = _dot_nt(q, ck) * scale
        m = jnp.maximum(jnp.max(sw, axis=-1, keepdims=True), jnp.max(sc, axis=-1, keepdims=True))
        pw = jnp.exp(sw - m)
        pc = jnp.exp(sc - m)
        l = jnp.sum(pw, axis=-1, keepdims=True) + jnp.sum(pc, axis=-1, keepdims=True)
        o = _dot(pw.astype(BF16), vw) + _dot(pc.astype(BF16), cv)
        y_ref[qb * nq:(qb + 1) * nq, :] = (o / l).astype(BF16)


def _attn_lat(z, cache_k, cache_v, bias_tab, layer, n_lat, row_blk0):
    qb, kb, vb = COL_QB // NA_DH, COL_KB // NA_DH, COL_VB // NA_DH
    past = cache_k.shape[3]
    cspec = pl.BlockSpec((None, None, None, past, NA_DH), lambda b, h: (b, layer, h, 0, 0))
    return pl.pallas_call(
        _attn_lat_kernel,
        grid=(n_lat, NA_HEADS),
        in_specs=[
            pl.BlockSpec((DEC_SEQ, NA_DH), lambda b, h: (row_blk0 + b, qb + h)),
            pl.BlockSpec((DEC_SEQ, NA_DH), lambda b, h: (row_blk0 + b, kb + h)),
            pl.BlockSpec((DEC_SEQ, NA_DH), lambda b, h: (row_blk0 + b, vb + h)),
            cspec, cspec,
            pl.BlockSpec((None, None, 3, NA_QROWS * GRID_W, NA_KROWS * GRID_W), lambda b, h: (layer, h, 0, 0, 0)),
        ],
        out_specs=pl.BlockSpec((DEC_SEQ, NA_DH), lambda b, h: (b, h)),
        out_shape=jax.ShapeDtypeStruct((n_lat * DEC_SEQ, NA_HEADS * NA_DH), BF16),
        compiler_params=_cparams(2, 40 * 2**20),
        name="attn_lat",
    )(z, z, z, cache_k, cache_v, bias_tab)


def _rope_tables(t_len):
    half = GLA_DK // 2
    nf = half // 2
    t = np.arange(t_len)
    inv = ROPE_BASE ** (-np.arange(nf, dtype=np.float32) / nf)
    ang_r = (t // GRID_W).astype(np.float32)[:, None] * inv
    ang_c = (t % GRID_W).astype(np.float32)[:, None] * inv
    cos = np.concatenate([np.cos(ang_r), np.cos(ang_r), np.cos(ang_c), np.cos(ang_c)], axis=-1)
    sin = np.concatenate([-np.sin(ang_r), np.sin(ang_r), -np.sin(ang_c), np.sin(ang_c)], axis=-1)
    return jnp.asarray(cos, F32), jnp.asarray(sin, F32)


def _log_sigmoid(x):
    return jnp.minimum(x, 0.0) - jnp.log(1.0 + jnp.exp(-jnp.abs(x)))


def _gla_kernel(*refs, t_len, rope, has_s0, hp):
    refs = list(refs)
    q_ref, k_ref, v_ref, r_ref, lr_ref, wgf_ref, wgb_ref, bgf_ref, bgb_ref, ng_ref = refs[:10]
    pos = 10
    if rope:
        cos_ref, sin_ref = refs[pos:pos + 2]
        pos += 2
    if has_s0:
        s0_ref = refs[pos]
        pos += 1
    y_ref, sfin_ref, qs, ks, gfs, gbs, of, ob = refs[pos:pos + 8]

    gc = GLA_CHUNK
    n_chunks = t_len // gc
    inv_temp = 1.0 / GLA_GATE_TEMP
    lrb = lr_ref[...].astype(BF16)
    if rope:
        lane = lax.broadcasted_iota(jnp.int32, (t_len, GLA_DK), 1)
        first = (lane % (GLA_DK // 2)) < (GLA_DK // 4)

        def swap(x):
            return jnp.where(first, pltpu.roll(x, GLA_DK - GLA_DK // 4, 1), pltpu.roll(x, GLA_DK // 4, 1))

    for hh in range(hp):
        kcols = slice(hh * GLA_DK, (hh + 1) * GLA_DK)
        q = q_ref[:, kcols].astype(F32)
        k = k_ref[:, kcols].astype(F32)
        if rope:
            q = q * cos_ref[...] + swap(q) * sin_ref[...]
            k = k * cos_ref[...] + swap(k) * sin_ref[...]
        qs[:, kcols] = q * (GLA_DK ** -0.5)
        ks[:, kcols] = k
        gfs[:, kcols] = _log_sigmoid(_dot(lrb, wgf_ref[:, kcols].astype(BF16)) + bgf_ref[:, kcols]) * inv_temp
        gbs[:, kcols] = _log_sigmoid(_dot(lrb, wgb_ref[:, kcols].astype(BF16)) + bgb_ref[:, kcols]) * inv_temp

    row = lax.broadcasted_iota(jnp.int32, (gc, gc), 0)
    col = lax.broadcasted_iota(jnp.int32, (gc, gc), 1)
    mid = gc // 2

    def chunk(c, hh, st, g_ref, causal):
        sl = pl.ds(pl.multiple_of(c * gc, gc), gc)
        kcols = slice(hh * GLA_DK, (hh + 1) * GLA_DK)
        vcols = slice(hh * GLA_DV, (hh + 1) * GLA_DV)
        mask = (row >= col) if causal else (row <= col)
        tri = mask.astype(BF16)
        qc, kc, vc, g = qs[sl, kcols], ks[sl, kcols], v_ref[sl, vcols], g_ref[sl, kcols]
        ghi, glo = _split_bf16(g)
        b = _dot(tri, ghi) + _dot(tri, glo)
        bmid = b[mid:mid + 1, :]
        tot = b[gc - 1:gc, :] if causal else b[0:1, :]
        q_in = qc * jnp.exp(b)
        qg = qc * jnp.exp(b - bmid)
        kg = kc * jnp.exp(bmid - b)
        kd = kc * jnp.exp(tot - b)
        sc = jnp.where(mask, _dot_nt(qg.astype(BF16), kg.astype(BF16)), 0.0)
        o = _dot(sc.astype(BF16), vc) + _dot_nt(q_in.astype(BF16), st.astype(BF16))
        st_new = jnp.exp(tot) * st + _dot_tn(vc, kd.astype(BF16))
        return o, st_new

    if has_s0:
        init = tuple(s0_ref[dr, hh].T for hh in range(hp) for dr in range(2))
    else:
        init = tuple(jnp.zeros((GLA_DV, GLA_DK), F32) for _ in range(2 * hp))

    def scan_step(i, carry):
        cb = n_chunks - 1 - i
        out = []
        for hh in range(hp):
            vcols = slice(hh * GLA_DV, (hh + 1) * GLA_DV)
            o_f, st_f = chunk(i, hh, carry[2 * hh], gfs, True)
            o_b, st_b = chunk(cb, hh, carry[2 * hh + 1], gbs, False)
            of[pl.ds(pl.multiple_of(i * gc, gc), gc), vcols] = o_f
            ob[pl.ds(pl.multiple_of(cb * gc, gc), gc), vcols] = o_b
            out += [st_f, st_b]
        return tuple(out)

    final = lax.fori_loop(0, n_chunks, scan_step, init)
    for hh in range(hp):
        sfin_ref[0, hh] = final[2 * hh].T
        sfin_ref[1, hh] = final[2 * hh + 1].T

    def finish(c, carry):
        sl = pl.ds(pl.multiple_of(c * gc, gc), gc)
        for hh in range(hp):
            vcols = slice(hh * GLA_DV, (hh + 1) * GLA_DV)
            o = of[sl, vcols] + ob[sl, vcols]
            o = o * lax.rsqrt(jnp.mean(o * o, axis=-1, keepdims=True) + EPS) * ng_ref[:, vcols]
            r = r_ref[sl, vcols].astype(F32)
            y_ref[sl, vcols] = (o * (r * jax.nn.sigmoid(r))).astype(BF16)
        return carry

    lax.fori_loop(0, n_chunks, finish, 0)


def _gla(z, lr, wgf, wgb, bgf, bgb, norm_g, n_batch, t_len, row_blk0, name, hp, rope_tabs=None, s0=None, layer=0):
    kw, vw = hp * GLA_DK, hp * GLA_DV
    assert GLA_HEADS % hp == 0 and COL_QC % kw == 0 and COL_KC % kw == 0 and COL_VC % vw == 0 and COL_RC % vw == 0
    qc, kc, vc, rc = COL_QC // kw, COL_KC // kw, COL_VC // vw, COL_RC // vw
    kdim = GLA_HEADS * GLA_DK
    in_specs = [
        pl.BlockSpec((t_len, kw), lambda b, h: (row_blk0 + b, qc + h)),
        pl.BlockSpec((t_len, kw), lambda b, h: (row_blk0 + b, kc + h)),
        pl.BlockSpec((t_len, vw), lambda b, h: (row_blk0 + b, vc + h)),
        pl.BlockSpec((t_len, vw), lambda b, h: (row_blk0 + b, rc + h)),
        pl.BlockSpec((t_len, 128), lambda b, h: (row_blk0 + b, 0)),
        pl.BlockSpec((128, kw), lambda b, h: (0, h)),
        pl.BlockSpec((128, kw), lambda b, h: (0, h)),
        pl.BlockSpec((1, kw), lambda b, h: (0, h)),
        pl.BlockSpec((1, kw), lambda b, h: (0, h)),
        pl.BlockSpec((1, vw), lambda b, h: (0, h)),
    ]
    args = [z, z, z, z, lr, wgf, wgb, bgf.reshape(1, kdim), bgb.reshape(1, kdim),
            norm_g.reshape(1, GLA_HEADS * GLA_DV)]
    if rope_tabs is not None:
        in_specs += [pl.BlockSpec((t_len, GLA_DK), lambda b, h: (0, 0))] * 2
        args += list(rope_tabs)
    if s0 is not None:
        in_specs.append(pl.BlockSpec((None, None, 2, hp, GLA_DK, GLA_DV), lambda b, h: (b, layer, 0, h, 0, 0)))
        args.append(s0)
    return pl.pallas_call(
        functools.partial(_gla_kernel, t_len=t_len, rope=rope_tabs is not None, has_s0=s0 is not None, hp=hp),
        grid=(n_batch, GLA_HEADS // hp),
        in_specs=in_specs,
        out_specs=[
            pl.BlockSpec((t_len, vw), lambda b, h: (b, h)),
            pl.BlockSpec((None, 2, hp, GLA_DK, GLA_DV), lambda b, h: (b, 0, h, 0, 0)),
        ],
        out_shape=[
            jax.ShapeDtypeStruct((n_batch * t_len, GLA_HEADS * GLA_DV), BF16),
            jax.ShapeDtypeStruct((n_batch, 2, GLA_HEADS, GLA_DK, GLA_DV), F32),
        ],
        scratch_shapes=[pltpu.VMEM((t_len, kw), F32)] * 4 + [pltpu.VMEM((t_len, vw), F32)] * 2,
        compiler_params=_cparams(2, 48 * 2**20),
        name=name,
    )(*args)


def _merge_kernel(ya_ref, yb_ref, yc_ref, wa_ref, wb_ref, wc_ref, ga_ref, gb_ref, gc_ref, o_ref):
    m = jax.nn.sigmoid(ga_ref[...].astype(F32)) * _dot(ya_ref[...], wa_ref[...])
    m += jax.nn.sigmoid(gb_ref[...].astype(F32)) * _dot(yb_ref[...], wb_ref[...])
    m += jax.nn.sigmoid(gc_ref[...].astype(F32)) * _dot(yc_ref[...], wc_ref[...])
    o_ref[...] = m.astype(BF16)


def _merge(ya, yb, yc, wa, wb, wc, z, layer, tm=512, tn=1024):
    m, kw = ya.shape
    d = wa.shape[-1]
    y_spec = pl.BlockSpec((tm, kw), lambda i, j: (i, 0))
    w_spec = pl.BlockSpec((None, kw, tn), lambda i, j: (layer, 0, j))

    def gate_spec(col):
        return pl.BlockSpec((tm, tn), lambda i, j: (i, col // tn + j))

    return pl.pallas_call(
        _merge_kernel,
        grid=(m // tm, d // tn),
        in_specs=[y_spec, y_spec, y_spec, w_spec, w_spec, w_spec,
                  gate_spec(COL_GA), gate_spec(COL_GB), gate_spec(COL_GC)],
        out_specs=pl.BlockSpec((tm, tn), lambda i, j: (i, j)),
        out_shape=jax.ShapeDtypeStruct((m, d), BF16),
        compiler_params=_cparams(2, 48 * 2**20),
        name="merge",
    )(ya, yb, yc, wa, wb, wc, z, z, z)


def _extract_top(x, count, out_ref, with_rank=False):
    rank = jnp.full(x.shape, float(PEER_TOPK), F32)
    for j in range(count):
        m = jnp.max(x, axis=0, keepdims=True)
        out_ref[j:j + 1, :] = m
        hit = x >= m
        if with_rank and j < PEER_TOPK:
            rank = jnp.where(hit, float(j), rank)
        if j + 1 < count:
            x = jnp.where(hit, NEG, x)
    return rank if with_rank else None


def _route_kernel(q_ref, k1_ref, k2_ref, cnt_ref, r2_ref, e2_ref, c1_ref, t1, t2, c):
    nk = PEER_NKEYS
    k = PEER_TOPK

    def scores(keys, qt):
        kh, kl = _split_bf16(keys)
        qh, ql = _split_bf16(qt)
        return _dot(kh, qh) + (_dot(kh, ql) + _dot(kl, qh))

    s1 = scores(k1_ref[...], q_ref[0:nk, :])
    s2 = scores(k2_ref[...], q_ref[nk:2 * nk, :])
    _extract_top(s1, k + 1, t1)
    rank2 = _extract_top(s2, k + 1, t2, with_rank=True)
    n = s1.shape[1]
    rid8 = lax.broadcasted_iota(jnp.int32, (8, n), 0)
    extra = jnp.where(rid8 == 0, t1[k:k + 1, :] + t2[0:1, :],
                      jnp.where(rid8 == 1, t1[0:1, :] + t2[k:k + 1, :], NEG))
    cand = jnp.concatenate(
        [t1[0:1, :] + t2[0:k, :]] + [t1[a:a + 1, :] + t2[0:8, :] for a in range(1, k)] + [extra], axis=0)
    _extract_top(cand, k + 1, c)
    tau = 0.5 * (c[k - 1:k, :] + c[k:k + 1, :])
    zsum = jnp.sum(jnp.exp(c[0:k, :] - c[0:1, :]), axis=0, keepdims=True)
    th = tau - s1
    cnt = jnp.zeros_like(th)
    for b in range(k):
        cnt = jnp.where(th <= t2[b:b + 1, :], float(b + 1), cnt)
    cnt_ref[...] = cnt
    r2_ref[...] = rank2.astype(BF16)
    e2_ref[...] = jnp.exp(s2 - t2[0:1, :]).astype(BF16)
    c1_ref[...] = jnp.exp(s1 - t1[0:1, :]) / zsum


def _peer_route(qt, k1, k2, nt=512):
    m = qt.shape[1]
    nk = PEER_NKEYS
    out_f32 = jax.ShapeDtypeStruct((PEER_HEADS, nk, m), F32)
    out_b16 = jax.ShapeDtypeStruct((PEER_HEADS, nk, m), BF16)
    ospec = pl.BlockSpec((None, nk, nt), lambda i, h: (h, 0, i))
    kspec = pl.BlockSpec((nk, nk), lambda i, h: (0, 0))
    return pl.pallas_call(
        _route_kernel,
        grid=(m // nt, PEER_HEADS),
        in_specs=[pl.BlockSpec((2 * nk, nt), lambda i, h: (h, i)), kspec, kspec],
        out_specs=[ospec] * 4,
        out_shape=[out_f32, out_b16, out_b16, out_f32],
        scratch_shapes=[pltpu.VMEM((24, nt), F32)] * 3,
        compiler_params=_cparams(2, 24 * 2**20),
        name="peer_route",
    )(qt, k1, k2)


PEER_TE = 1024
PEER_RBLK = 4


def _peer_kernel(ht_ref, cnt_ref, r2_in_ref, e2_in_ref, c1_ref, u_ref, vt_ref, o_ref,
                 act_ref, g_ref, r2_ref, e2_ref, *, te, nt):
    j = pl.program_id(1)
    nk = PEER_NKEYS
    groups = te // nk

    @pl.when(j == 0)
    def _():
        o_ref[...] = jnp.zeros_like(o_ref)
        r2_ref[...] = r2_in_ref[...]
        e2_ref[...] = e2_in_ref[...]

    act_ref[...] = _dot(u_ref[...], ht_ref[...]).astype(BF16)

    def row_bf16(ref, h, r, lanes):
        row = jnp.broadcast_to(ref[h, r:r + 1, lanes], (16, 128)).astype(BF16)
        return pltpu.repeat(row, nk // 16, axis=0)

    for lb in range(nt // 128):
        lanes = slice(lb * 128, (lb + 1) * 128)
        for rb in range(groups // PEER_RBLK):
            ws = [jnp.zeros((nk, 128), BF16) for _ in range(PEER_RBLK)]
            for h in range(PEER_HEADS):
                r2t = r2_ref[h * nk:(h + 1) * nk, lanes]
                e2t = e2_ref[h * nk:(h + 1) * nk, lanes]
                for rr in range(PEER_RBLK):
                    r = rb * PEER_RBLK + rr
                    hit = r2t < row_bf16(cnt_ref, h, r, lanes)
                    ws[rr] = ws[rr] + jnp.where(hit, e2t * row_bf16(c1_ref, h, r, lanes), 0.0)
            for rr in range(PEER_RBLK):
                rows = slice((rb * PEER_RBLK + rr) * nk, (rb * PEER_RBLK + rr + 1) * nk)
                g_ref[rows, lanes] = jax.nn.gelu(act_ref[rows, lanes]) * ws[rr]

    o_ref[...] += _dot(vt_ref[...], g_ref[...])


def _peer(ht, cnt, r2, e2, c1, u, vt, layer, nt=512):
    d, m = ht.shape
    te = PEER_TE
    nk = PEER_NKEYS
    groups = te // nk
    assert groups == 8, "first-key rows of a tile must fill one sublane group"
    rspec = pl.BlockSpec((PEER_HEADS * nk, nt), lambda i, j: (0, i))
    gspec = pl.BlockSpec((PEER_HEADS, groups, nt), lambda i, j: (0, j, i))
    est = (2 * (d * nt * 2 + 2 * PEER_HEADS * nk * nt * 2 + 2 * te * d * 2 + d * nt * 4)
           + 2 * te * nt * 2 + te * nt * 4 + d * nt * 4)
    return pl.pallas_call(
        functools.partial(_peer_kernel, te=te, nt=nt),
        grid=(m // nt, N_EXPERTS // te),
        in_specs=[
            pl.BlockSpec((d, nt), lambda i, j: (0, i)),
            gspec, rspec, rspec, gspec,
            pl.BlockSpec((None, te, d), lambda i, j: (layer, j, 0)),
            pl.BlockSpec((None, None, d, te), lambda i, j: (layer, j, 0, 0)),
        ],
        out_specs=pl.BlockSpec((d, nt), lambda i, j: (0, i)),
        out_shape=jax.ShapeDtypeStruct((d, m), F32),
        scratch_shapes=[pltpu.VMEM((te, nt), BF16), pltpu.VMEM((te, nt), BF16),
                        pltpu.VMEM((PEER_HEADS * nk, nt), BF16), pltpu.VMEM((PEER_HEADS * nk, nt), BF16)],
        compiler_params=_cparams(2, est + 2 * PEER_HEADS * nk * nt * 2 + 6 * 2**20),
        name="peer_experts",
    )(ht, cnt, r2.reshape(PEER_HEADS * nk, m), e2.reshape(PEER_HEADS * nk, m), c1, u, vt)


def kernel(x_prompt, x_sample, cache_k, cache_v, state_gla, c, c_ctx, w_ada, b_ada, w_in, a_ln_g, a_ln_b, a_ws, a_bs, na_rpb, gla_wg_f, gla_bg_f, gla_wg_b, gla_bg_b, gla_norm_g, w_br_a, w_br_b, w_br_c, w_out, ln1_g, ln1_b, ln2_g, ln2_b, peer_wq, peer_k1, peer_k2, peer_u, peer_v):
    n_ctx, n_lat = x_prompt.shape[0], x_sample.shape[0]
    n_layers = w_in.shape[0]
    d = D_MODEL
    mc, ml = n_ctx * SEQ, n_lat * DEC_SEQ
    assert mc % DEC_SEQ == 0, "latent row blocks must start on a DEC_SEQ boundary of the shared token axis"
    assert n_lat + 1 <= 16

    cc = jnp.zeros((16, d), F32).at[0].set(c_ctx).at[1:1 + n_lat].set(c)
    mod = _ada(cc, w_ada, b_ada).reshape(n_layers, 16, N_MOD, 1, d)

    def mod_vec(layer, which):
        return mod[layer, :, which]

    w_main = jnp.concatenate([w_in[:, :, :LR_START], w_in[:, :, LR_END:]], axis=-1).astype(BF16)
    w_lr = jnp.pad(w_in[:, :, LR_START:LR_END], ((0, 0), (0, 0), (0, 128 - (LR_END - LR_START)))).astype(BF16)
    wgf = jnp.pad(gla_wg_f, ((0, 0), (0, 128 - GLA_RANK), (0, 0)))
    wgb = jnp.pad(gla_wg_b, ((0, 0), (GLA_RANK, 128 - 2 * GLA_RANK), (0, 0)))
    wbr_a, wbr_b, wbr_c = w_br_a.astype(BF16), w_br_b.astype(BF16), w_br_c.astype(BF16)
    w_o = w_out.astype(BF16)
    wq_t = jnp.swapaxes(peer_wq, 1, 2).astype(BF16)
    u_b = peer_u.astype(BF16)
    v_t = jnp.swapaxes(peer_v.reshape(n_layers, N_EXPERTS // PEER_TE, PEER_TE, d), 2, 3).astype(BF16)
    rope_tabs = _rope_tables(DEC_SEQ)
    bias_tab = _na_bias_tables(na_rpb)

    x = jnp.concatenate([x_prompt.reshape(mc, d), x_sample.reshape(ml, d)], axis=0)
    h = _modulate(x, mod_vec(0, 0), mod_vec(0, 1), mc)

    new_k, new_v, new_s = [], [], []
    for l in range(n_layers):
        z = _matmul(h, w_main, BF16, 512, 2048, "in_proj", b_resident=True, layer=l)
        lr = _matmul(h, w_lr[l], F32, 512, 128, "lr_proj")
        ya = _mixer_a(z, a_ln_g[l], a_ln_b[l], a_ws[l], a_bs[l])
        yb_c, k_l, v_l = _attn_ctx(z, n_ctx)
        yb_l = _attn_lat(z, cache_k, cache_v, bias_tab, l, n_lat, mc // DEC_SEQ)
        gla_args = (wgf[l], wgb[l], gla_bg_f[l], gla_bg_b[l], gla_norm_g[l])
        yc_c, s_l = _gla(z, lr, *gla_args, n_ctx, SEQ, 0, "gla_ctx", 4)
        yc_l, _ = _gla(z, lr, *gla_args, n_lat, DEC_SEQ, mc // DEC_SEQ, "gla_lat", 2,
                       rope_tabs=rope_tabs, s0=state_gla, layer=l)
        new_k.append(k_l)
        new_v.append(v_l)
        new_s.append(s_l)
        yb = jnp.concatenate([yb_c, yb_l], axis=0)
        yc = jnp.concatenate([yc_c, yc_l], axis=0)
        mrg = _merge(ya, yb, yc, wbr_a, wbr_b, wbr_c, z, l)
        y = _matmul(mrg, w_o, F32, 512, 1024, "out_proj", layer=l)
        x, h2t = _res_ln(x, y, mod_vec(l, 2), ln1_g[l], ln1_b[l], mod_vec(l, 3), mod_vec(l, 4), mc, "res_ln1",
                         h_transposed=True)
        qt = _matmul(wq_t, h2t, F32, 512, 1024, "peer_q", layer=l)
        cnt, r2, e2, c1 = _peer_route(qt, peer_k1[l], peer_k2[l])
        pt = _peer(h2t, cnt, r2, e2, c1, u_b, v_t, l)
        nxt = min(l + 1, n_layers - 1)
        x, h = _res_ln(x, pt, mod_vec(l, 5), ln2_g[l], ln2_b[l], mod_vec(nxt, 0), mod_vec(nxt, 1), mc, "res_ln2",
                       y_transposed=True)

    y_prompt = x[:mc].reshape(n_ctx, SEQ, d)
    y_sample = x[mc:].reshape(n_lat, DEC_SEQ, d)
    return (y_prompt, y_sample, jnp.stack(new_k, axis=1), jnp.stack(new_v, axis=1), jnp.stack(new_s, axis=1))
```

```python
import functools

import numpy as np
import jax
import jax.numpy as jnp
from jax import lax
from jax.experimental import pallas as pl
from jax.experimental.pallas import tpu as pltpu

F32 = jnp.float32
BF16 = jnp.bfloat16

D_MODEL = 2048
SEQ = 256
DEC_SEQ = 2048
GRID_W = 64
CHUNK = 128
A_GROUPS = 8
A_WIDTH = 1024
NA_HEADS = 8
NA_DH = 128
NA_WIN_H = 8
NA_WIN_W = 16
GLA_HEADS = 4
GLA_DK = 128
GLA_DV = 256
GLA_RANK = 16
GLA_GATE_TEMP = 16.0
GLA_CHUNK = 64
ROPE_BASE = 10000.0
PEER_HEADS = 8
PEER_NKEYS = 128
PEER_TOPK = 16
N_EXPERTS = PEER_NKEYS * PEER_NKEYS
N_MOD = 6
ALPHA = 8.0 ** 0.25
EPS = 1e-5
NEG = -1e30

Z_WIDTH = 14336
COL_A = 0
COL_QB, COL_KB, COL_VB = 2048, 3072, 4096
COL_QC, COL_KC, COL_VC, COL_RC = 5120, 5632, 6144, 7168
COL_GA, COL_GB, COL_GC = 8192, 10240, 12288
LR_START, LR_END = 8192, 8224

VMEM_CAP = 56 * 1024 * 1024


def _cparams(n_axes, vmem_bytes, flags=None):
    return pltpu.CompilerParams(
        dimension_semantics=("arbitrary",) * n_axes,
        vmem_limit_bytes=min(int(vmem_bytes), VMEM_CAP),
        flags=flags,
    )


def _mod_row(i, tm, n_ctx_rows):
    start = i * tm
    return jnp.where(start < n_ctx_rows, 0, 1 + (start - n_ctx_rows) // DEC_SEQ)


def _dot(a, b):
    return jnp.dot(a, b, preferred_element_type=F32)


def _dot_nt(a, b):
    return lax.dot_general(a, b, (((1,), (1,)), ((), ())), preferred_element_type=F32)


def _dot_tn(a, b):
    return lax.dot_general(a, b, (((0,), (0,)), ((), ())), preferred_element_type=F32)


def _split_bf16(x):
    hi = x.astype(BF16)
    lo = (x - hi.astype(F32)).astype(BF16)
    return hi, lo


def _ada_kernel(c_ref, w_ref, b_ref, o_ref):
    c = c_ref[...]
    a = (c * jax.nn.sigmoid(c)).astype(BF16)
    o_ref[...] = _dot(a, w_ref[...].astype(BF16)) + b_ref[...]


def _ada(cc, w_ada, b_ada):
    n_layers, d, n = w_ada.shape
    rows = cc.shape[0]
    tn = 1024
    return pl.pallas_call(
        _ada_kernel,
        grid=(n_layers, n // tn),
        in_specs=[
            pl.BlockSpec((rows, d), lambda l, j: (0, 0)),
            pl.BlockSpec((None, d, tn), lambda l, j: (l, 0, j)),
            pl.BlockSpec((None, 1, tn), lambda l, j: (l, 0, j)),
        ],
        out_specs=pl.BlockSpec((None, rows, tn), lambda l, j: (l, 0, j)),
        out_shape=jax.ShapeDtypeStruct((n_layers, rows, n), F32),
        compiler_params=_cparams(2, 2 * d * tn * 4 + 12 * 2**20),
        name="ada_mod",
    )(cc, w_ada, b_ada.reshape(n_layers, 1, n))


def _mm_kernel(a_ref, b_ref, o_ref):
    o_ref[...] = _dot(a_ref[...].astype(BF16), b_ref[...].astype(BF16)).astype(o_ref.dtype)


def _matmul(a, b, out_dtype, tm, tn, name, b_resident=False, layer=None):
    m, k = a.shape[-2:]
    n = b.shape[-1]
    assert m % tm == 0 and n % tn == 0, (m, n, tm, tn)
    est = 2 * (tm * k * a.dtype.itemsize + k * tn * b.dtype.itemsize + tm * tn * 4) + tm * tn * 8
    if b_resident:
        grid = (n // tn, m // tm)
        a_idx, b_idx, o_map = (lambda j, i: (i, 0)), (lambda j, i: (0, j)), (lambda j, i: (i, j))
    else:
        grid = (m // tm, n // tn)
        a_idx, b_idx, o_map = (lambda i, j: (i, 0)), (lambda i, j: (0, j)), (lambda i, j: (i, j))

    def spec(x, block, idx):
        if x.ndim == 2:
            return pl.BlockSpec(block, idx)
        return pl.BlockSpec((None,) + block, lambda *g: (layer,) + idx(*g))

    return pl.pallas_call(
        _mm_kernel,
        grid=grid,
        in_specs=[spec(a, (tm, k), a_idx), spec(b, (k, tn), b_idx)],
        out_specs=pl.BlockSpec((tm, tn), o_map),
        out_shape=jax.ShapeDtypeStruct((m, n), out_dtype),
        compiler_params=_cparams(2, est + 8 * 2**20),
        name=name,
    )(a, b)


def _modulate_kernel(x_ref, sh_ref, sc_ref, h_ref):
    h_ref[...] = (x_ref[...] * (1.0 + sc_ref[...]) + sh_ref[...]).astype(BF16)


def _modulate(x, sh, sc, n_ctx_rows, tm=256):
    m, d = x.shape
    vec = pl.BlockSpec((None, 1, d), lambda i: (_mod_row(i, tm, n_ctx_rows), 0, 0))
    return pl.pallas_call(
        _modulate_kernel,
        grid=(m // tm,),
        in_specs=[pl.BlockSpec((tm, d), lambda i: (i, 0)), vec, vec],
        out_specs=pl.BlockSpec((tm, d), lambda i: (i, 0)),
        out_shape=jax.ShapeDtypeStruct((m, d), BF16),
        compiler_params=_cparams(1, 32 * 2**20),
        name="modulate",
    )(x, sh, sc)


def _res_ln_kernel(x_ref, y_ref, g_ref, lg_ref, lb_ref, sh_ref, sc_ref, xo_ref, h_ref, *, y_transposed, h_transposed):
    y = y_ref[...].astype(F32)
    if y_transposed:
        y = y.T
    t = ALPHA * x_ref[...] + g_ref[...] * y
    mu = jnp.mean(t, axis=-1, keepdims=True)
    tc = t - mu
    var = jnp.mean(tc * tc, axis=-1, keepdims=True)
    xn = tc * lax.rsqrt(var + EPS) * lg_ref[...] + lb_ref[...]
    xo_ref[...] = xn
    h = xn * (1.0 + sc_ref[...]) + sh_ref[...]
    h_ref[...] = (h.T if h_transposed else h).astype(BF16)


def _res_ln(x, y, g, ln_g, ln_b, sh, sc, n_ctx_rows, name, tm=256, y_transposed=False, h_transposed=False,
            rows=None):
    d = x.shape[1]
    start, m = (0, x.shape[0]) if rows is None else rows
    t0 = start // tm
    assert start % tm == 0 and m % tm == 0
    vec = pl.BlockSpec((None, 1, d), lambda i: (_mod_row(i + t0, tm, n_ctx_rows), 0, 0))
    par = pl.BlockSpec((1, d), lambda i: (0, 0))
    row_in = pl.BlockSpec((tm, d), lambda i: (i + t0, 0))
    col_in = pl.BlockSpec((d, tm), lambda i: (0, i + t0))
    row = pl.BlockSpec((tm, d), lambda i: (i, 0))
    col = pl.BlockSpec((d, tm), lambda i: (0, i))
    return pl.pallas_call(
        functools.partial(_res_ln_kernel, y_transposed=y_transposed, h_transposed=h_transposed),
        grid=(m // tm,),
        in_specs=[row_in, col_in if y_transposed else row_in, vec, par, par, vec, vec],
        out_specs=[row, col if h_transposed else row],
        out_shape=[jax.ShapeDtypeStruct((m, d), F32),
                   jax.ShapeDtypeStruct((d, m) if h_transposed else (m, d), BF16)],
        compiler_params=_cparams(1, 40 * 2**20),
        name=name,
    )(x, y, g, ln_g.reshape(1, d), ln_b.reshape(1, d), sh, sc)


def _mixer_a_kernel(a_ref, lg_ref, lb_ref, ws_ref, bs_ref, y_ref, *, n_chunks):
    gw = A_WIDTH // A_GROUPS
    for c in range(n_chunks):
        rows = slice(c * CHUNK, (c + 1) * CHUNK)
        g = jax.nn.gelu(a_ref[rows, :].astype(F32))
        u = g[:, :A_WIDTH]
        v = g[:, A_WIDTH:]
        mu = jnp.mean(v, axis=-1, keepdims=True)
        vc = v - mu
        var = jnp.mean(vc * vc, axis=-1, keepdims=True)
        v = vc * lax.rsqrt(var + EPS) * lg_ref[...] + lb_ref[...]
        for gi in range(A_GROUPS):
            cols = slice(gi * gw, (gi + 1) * gw)
            sp = _dot(ws_ref[gi].astype(BF16), v[:, cols].astype(BF16)) + bs_ref[gi]
            y_ref[rows, cols] = (u[:, cols] * sp).astype(BF16)


def _mixer_a(z, ln_g, ln_b, ws, bs, tm=256):
    m = z.shape[0]
    gw = A_WIDTH // A_GROUPS
    bs_b = jnp.broadcast_to(bs[:, :, None], (A_GROUPS, CHUNK, gw))
    return pl.pallas_call(
        functools.partial(_mixer_a_kernel, n_chunks=tm // CHUNK),
        grid=(m // tm,),
        in_specs=[
            pl.BlockSpec((tm, 2 * A_WIDTH), lambda i: (i, COL_A // (2 * A_WIDTH))),
            pl.BlockSpec((1, A_WIDTH), lambda i: (0, 0)),
            pl.BlockSpec((1, A_WIDTH), lambda i: (0, 0)),
            pl.BlockSpec((A_GROUPS, CHUNK, CHUNK), lambda i: (0, 0, 0)),
            pl.BlockSpec((A_GROUPS, CHUNK, gw), lambda i: (0, 0, 0)),
        ],
        out_specs=pl.BlockSpec((tm, A_WIDTH), lambda i: (i, 0)),
        out_shape=jax.ShapeDtypeStruct((m, A_WIDTH), BF16),
        compiler_params=_cparams(1, 32 * 2**20),
        name="mixer_a",
    )(z, ln_g.reshape(1, A_WIDTH), ln_b.reshape(1, A_WIDTH), ws, bs_b)


def _attn_ctx_kernel(q_ref, k_ref, v_ref, y_ref, ko_ref, vo_ref):
    for h in range(NA_HEADS):
        cols = slice(h * NA_DH, (h + 1) * NA_DH)
        q, k, v = q_ref[:, cols], k_ref[:, cols], v_ref[:, cols]
        s = _dot_nt(q, k) * (NA_DH ** -0.5)
        m = jnp.max(s, axis=-1, keepdims=True)
        p = jnp.exp(s - m)
        l = jnp.sum(p, axis=-1, keepdims=True)
        o = _dot(p.astype(BF16), v) / l
        y_ref[:, cols] = o.astype(BF16)
        ko_ref[h] = k.astype(F32)
        vo_ref[h] = v.astype(F32)


def _attn_ctx(z, n_ctx):
    width = NA_HEADS * NA_DH
    assert COL_QB % width == 0 and COL_KB % width == 0 and COL_VB % width == 0
    cache = jax.ShapeDtypeStruct((n_ctx, NA_HEADS, SEQ, NA_DH), F32)
    cache_spec = pl.BlockSpec((None, NA_HEADS, SEQ, NA_DH), lambda b: (b, 0, 0, 0))
    return pl.pallas_call(
        _attn_ctx_kernel,
        grid=(n_ctx,),
        in_specs=[
            pl.BlockSpec((SEQ, width), lambda b: (b, COL_QB // width)),
            pl.BlockSpec((SEQ, width), lambda b: (b, COL_KB // width)),
            pl.BlockSpec((SEQ, width), lambda b: (b, COL_VB // width)),
        ],
        out_specs=[pl.BlockSpec((SEQ, width), lambda b: (b, 0)), cache_spec, cache_spec],
        out_shape=[jax.ShapeDtypeStruct((n_ctx * SEQ, width), BF16), cache, cache],
        compiler_params=_cparams(1, 24 * 2**20),
        name="attn_ctx",
    )(z, z, z)


NA_QROWS = 4
NA_KROWS = 12
NA_NQB = (DEC_SEQ // GRID_W) // NA_QROWS


def _na_block(qb):
    n_rows = DEC_SEQ // GRID_W
    k0 = min(max(NA_QROWS * qb - NA_WIN_H // 2, 0), n_rows - NA_KROWS)
    case = 0 if qb == 0 else (2 if qb == NA_NQB - 1 else 1)
    return k0, case


def _rpb_expand_kernel(rp_ref, oh_ref, neg_ref, o_ref):
    r = rp_ref[...]
    hi = r.astype(BF16)
    r1 = r - hi.astype(F32)
    mid = r1.astype(BF16)
    lo = (r1 - mid.astype(F32)).astype(BF16)
    oh = oh_ref[...]
    o_ref[...] = (_dot(hi, oh) + _dot(mid, oh)) + (_dot(lo, oh) + neg_ref[...])


def _na_bias_tables(rpb):
    n_layers, n_heads, n_dr, n_dc = rpb.shape
    cols = np.arange(GRID_W)
    cs = np.clip(cols - NA_WIN_W // 2, 0, GRID_W - NA_WIN_W)
    ok_c = (cols[None, :] >= cs[:, None]) & (cols[None, :] < cs[:, None] + NA_WIN_W)
    dc = np.clip(cols[None, :] - cols[:, None], -(NA_WIN_W - 1), NA_WIN_W - 1) + NA_WIN_W - 1
    onehot = (np.arange(32)[:, None] == dc.reshape(1, -1)).astype(np.float32)
    negmask = np.where(ok_c.reshape(1, -1), 0.0, NEG).astype(np.float32)
    rows = n_layers * n_heads * n_dr
    rows_pad = -(-rows // 128) * 128
    rp = jnp.pad(rpb.reshape(rows, n_dc), ((0, rows_pad - rows), (0, 32 - n_dc)))
    t = pl.pallas_call(
        _rpb_expand_kernel,
        grid=(rows_pad // 128,),
        in_specs=[pl.BlockSpec((128, 32), lambda i: (i, 0)),
                  pl.BlockSpec((32, GRID_W * GRID_W), lambda i: (0, 0)),
                  pl.BlockSpec((1, GRID_W * GRID_W), lambda i: (0, 0))],
        out_specs=pl.BlockSpec((128, GRID_W * GRID_W), lambda i: (i, 0)),
        out_shape=jax.ShapeDtypeStruct((rows_pad, GRID_W * GRID_W), F32),
        compiler_params=_cparams(1, 16 * 2**20),
        name="rpb_expand",
    )(rp, jnp.asarray(onehot, BF16), jnp.asarray(negmask))
    t = t[:rows].reshape(n_layers, n_heads, n_dr, GRID_W, GRID_W)
    n_rows = DEC_SEQ // GRID_W
    neg_blk = jnp.full((n_layers, n_heads, GRID_W, GRID_W), NEG, F32)
    cases = []
    for qb in (0, 1, NA_NQB - 1):
        k0, _ = _na_block(qb)
        q_rows = []
        for qr in range(NA_QROWS):
            r = NA_QROWS * qb + qr
            rs = min(max(r - NA_WIN_H // 2, 0), n_rows - NA_WIN_H)
            blks = []
            for j in range(NA_KROWS):
                kr = k0 + j
                blks.append(t[:, :, kr - r + NA_WIN_H - 1] if rs <= kr < rs + NA_WIN_H else neg_blk)
            q_rows.append(jnp.concatenate(blks, axis=-1))
        cases.append(jnp.concatenate(q_rows, axis=-2))
    return jnp.stack(cases, axis=2)


def _attn_lat_kernel(q_ref, k_ref, v_ref, ck_ref, cv_ref, bias_ref, y_ref):
    scale = NA_DH ** -0.5
    ck = ck_ref[...].astype(BF16)
    cv = cv_ref[...].astype(BF16)
    nq = NA_QROWS * GRID_W
    nk = NA_KROWS * GRID_W
    for qb in range(NA_NQB):
        k0, case = _na_block(qb)
        q = q_ref[qb * nq:(qb + 1) * nq, :]
        kw = k_ref[k0 * GRID_W:k0 * GRID_W + nk, :]
        vw = v_ref[k0 * GRID_W:k0 * GRID_W + nk, :]
        sw = _dot_nt(q, kw) * scale + bias_ref[case]
        sc = _dot_nt(q, ck) * scale
        m = jnp.maximum(jnp.max(sw, axis=-1, keepdims=True), jnp.max(sc, axis=-1, keepdims=True))
        pw = jnp.exp(sw - m)
        pc = jnp.exp(sc - m)
        l = jnp.sum(pw, axis=-1, keepdims=True) + jnp.sum(pc, axis=-1, keepdims=True)
        o = _dot(pw.astype(BF16), vw) + _dot(pc.astype(BF16), cv)
        y_ref[qb * nq:(qb + 1) * nq, :] = (o / l).astype(BF16)


def _attn_lat(z, cache_k, cache_v, bias_tab, layer, n_lat, row_blk0):
    qb, kb, vb = COL_QB // NA_DH, COL_KB // NA_DH, COL_VB // NA_DH
    past = cache_k.shape[3]
    cspec = pl.BlockSpec((None, None, None, past, NA_DH), lambda b, h: (b, layer, h, 0, 0))
    return pl.pallas_call(
        _attn_lat_kernel,
        grid=(n_lat, NA_HEADS),
        in_specs=[
            pl.BlockSpec((DEC_SEQ, NA_DH), lambda b, h: (row_blk0 + b, qb + h)),
            pl.BlockSpec((DEC_SEQ, NA_DH), lambda b, h: (row_blk0 + b, kb + h)),
            pl.BlockSpec((DEC_SEQ, NA_DH), lambda b, h: (row_blk0 + b, vb + h)),
            cspec, cspec,
            pl.BlockSpec((None, None, 3, NA_QROWS * GRID_W, NA_KROWS * GRID_W), lambda b, h: (layer, h, 0, 0, 0)),
        ],
        out_specs=pl.BlockSpec((DEC_SEQ, NA_DH), lambda b, h: (b, h)),
        out_shape=jax.ShapeDtypeStruct((n_lat * DEC_SEQ, NA_HEADS * NA_DH), BF16),
        compiler_params=_cparams(2, 40 * 2**20),
        name="attn_lat",
    )(z, z, z, cache_k, cache_v, bias_tab)


def _rope_tables(t_len):
    half = GLA_DK // 2
    nf = half // 2
    t = np.arange(t_len)
    inv = ROPE_BASE ** (-np.arange(nf, dtype=np.float32) / nf)
    ang_r = (t // GRID_W).astype(np.float32)[:, None] * inv
    ang_c = (t % GRID_W).astype(np.float32)[:, None] * inv
    cos = np.concatenate([np.cos(ang_r), np.cos(ang_r), np.cos(ang_c), np.cos(ang_c)], axis=-1)
    sin = np.concatenate([-np.sin(ang_r), np.sin(ang_r), -np.sin(ang_c), np.sin(ang_c)], axis=-1)
    return jnp.asarray(cos, F32), jnp.asarray(sin, F32)


def _log_sigmoid(x):
    return jnp.minimum(x, 0.0) - jnp.log(1.0 + jnp.exp(-jnp.abs(x)))


def _gla_kernel(*refs, t_len, rope, has_s0, hp):
    refs = list(refs)
    q_ref, k_ref, v_ref, r_ref, lr_ref, wgf_ref, wgb_ref, bgf_ref, bgb_ref, ng_ref = refs[:10]
    pos = 10
    if rope:
        cos_ref, sin_ref = refs[pos:pos + 2]
        pos += 2
    if has_s0:
        s0_ref = refs[pos]
        pos += 1
    y_ref, sfin_ref, qs, ks, gfs, gbs, of, ob = refs[pos:pos + 8]

    gc = GLA_CHUNK
    n_chunks = t_len // gc
    inv_temp = 1.0 / GLA_GATE_TEMP
    lrb = lr_ref[...].astype(BF16)
    if rope:
        lane = lax.broadcasted_iota(jnp.int32, (t_len, GLA_DK), 1)
        first = (lane % (GLA_DK // 2)) < (GLA_DK // 4)

        def swap(x):
            return jnp.where(first, pltpu.roll(x, GLA_DK - GLA_DK // 4, 1), pltpu.roll(x, GLA_DK // 4, 1))

    for hh in range(hp):
        kcols = slice(hh * GLA_DK, (hh + 1) * GLA_DK)
        q = q_ref[:, kcols].astype(F32)
        k = k_ref[:, kcols].astype(F32)
        if rope:
            q = q * cos_ref[...] + swap(q) * sin_ref[...]
            k = k * cos_ref[...] + swap(k) * sin_ref[...]
        qs[:, kcols] = q * (GLA_DK ** -0.5)
        ks[:, kcols] = k
        gfs[:, kcols] = _log_sigmoid(_dot(lrb, wgf_ref[:, kcols].astype(BF16)) + bgf_ref[:, kcols]) * inv_temp
        gbs[:, kcols] = _log_sigmoid(_dot(lrb, wgb_ref[:, kcols].astype(BF16)) + bgb_ref[:, kcols]) * inv_temp

    row = lax.broadcasted_iota(jnp.int32, (gc, gc), 0)
    col = lax.broadcasted_iota(jnp.int32, (gc, gc), 1)
    mid = gc // 2

    def chunk(c, hh, st, g_ref, causal):
        sl = pl.ds(pl.multiple_of(c * gc, gc), gc)
        kcols = slice(hh * GLA_DK, (hh + 1) * GLA_DK)
        vcols = slice(hh * GLA_DV, (hh + 1) * GLA_DV)
        mask = (row >= col) if causal else (row <= col)
        tri = mask.astype(BF16)
        qc, kc, vc, g = qs[sl, kcols], ks[sl, kcols], v_ref[sl, vcols], g_ref[sl, kcols]
        ghi, glo = _split_bf16(g)
        b = _dot(tri, ghi) + _dot(tri, glo)
        bmid = b[mid:mid + 1, :]
        tot = b[gc - 1:gc, :] if causal else b[0:1, :]
        q_in = qc * jnp.exp(b)
        qg = qc * jnp.exp(b - bmid)
        kg = kc * jnp.exp(bmid - b)
        kd = kc * jnp.exp(tot - b)
        sc = jnp.where(mask, _dot_nt(qg.astype(BF16), kg.astype(BF16)), 0.0)
        o = _dot(sc.astype(BF16), vc) + _dot_nt(q_in.astype(BF16), st.astype(BF16))
        st_new = jnp.exp(tot) * st + _dot_tn(vc, kd.astype(BF16))
        return o, st_new

    if has_s0:
        init = tuple(s0_ref[dr, hh].T for hh in range(hp) for dr in range(2))
    else:
        init = tuple(jnp.zeros((GLA_DV, GLA_DK), F32) for _ in range(2 * hp))

    def scan_step(i, carry):
        cb = n_chunks - 1 - i
        out = []
        for hh in range(hp):
            vcols = slice(hh * GLA_DV, (hh + 1) * GLA_DV)
            o_f, st_f = chunk(i, hh, carry[2 * hh], gfs, True)
            o_b, st_b = chunk(cb, hh, carry[2 * hh + 1], gbs, False)
            of[pl.ds(pl.multiple_of(i * gc, gc), gc), vcols] = o_f
            ob[pl.ds(pl.multiple_of(cb * gc, gc), gc), vcols] = o_b
            out += [st_f, st_b]
        return tuple(out)

    final = lax.fori_loop(0, n_chunks, scan_step, init)
    for hh in range(hp):
        sfin_ref[0, hh] = final[2 * hh].T
        sfin_ref[1, hh] = final[2 * hh + 1].T

    def finish(c, carry):
        sl = pl.ds(pl.multiple_of(c * gc, gc), gc)
        for hh in range(hp):
            vcols = slice(hh * GLA_DV, (hh + 1) * GLA_DV)
            o = of[sl, vcols] + ob[sl, vcols]
            o = o * lax.rsqrt(jnp.mean(o * o, axis=-1, keepdims=True) + EPS) * ng_ref[:, vcols]
            r = r_ref[sl, vcols].astype(F32)
            y_ref[sl, vcols] = (o * (r * jax.nn.sigmoid(r))).astype(BF16)
        return carry

    lax.fori_loop(0, n_chunks, finish, 0)


def _gla(z, lr, wgf, wgb, bgf, bgb, norm_g, n_batch, t_len, row_blk0, name, hp, rope_tabs=None, s0=None, layer=0):
    kw, vw = hp * GLA_DK, hp * GLA_DV
    assert GLA_HEADS % hp == 0 and COL_QC % kw == 0 and COL_KC % kw == 0 and COL_VC % vw == 0 and COL_RC % vw == 0
    qc, kc, vc, rc = COL_QC // kw, COL_KC // kw, COL_VC // vw, COL_RC // vw
    kdim = GLA_HEADS * GLA_DK
    in_specs = [
        pl.BlockSpec((t_len, kw), lambda b, h: (row_blk0 + b, qc + h)),
        pl.BlockSpec((t_len, kw), lambda b, h: (row_blk0 + b, kc + h)),
        pl.BlockSpec((t_len, vw), lambda b, h: (row_blk0 + b, vc + h)),
        pl.BlockSpec((t_len, vw), lambda b, h: (row_blk0 + b, rc + h)),
        pl.BlockSpec((t_len, 128), lambda b, h: (row_blk0 + b, 0)),
        pl.BlockSpec((128, kw), lambda b, h: (0, h)),
        pl.BlockSpec((128, kw), lambda b, h: (0, h)),
        pl.BlockSpec((1, kw), lambda b, h: (0, h)),
        pl.BlockSpec((1, kw), lambda b, h: (0, h)),
        pl.BlockSpec((1, vw), lambda b, h: (0, h)),
    ]
    args = [z, z, z, z, lr, wgf, wgb, bgf.reshape(1, kdim), bgb.reshape(1, kdim),
            norm_g.reshape(1, GLA_HEADS * GLA_DV)]
    if rope_tabs is not None:
        in_specs += [pl.BlockSpec((t_len, GLA_DK), lambda b, h: (0, 0))] * 2
        args += list(rope_tabs)
    if s0 is not None:
        in_specs.append(pl.BlockSpec((None, None, 2, hp, GLA_DK, GLA_DV), lambda b, h: (b, layer, 0, h, 0, 0)))
        args.append(s0)
    return pl.pallas_call(
        functools.partial(_gla_kernel, t_len=t_len, rope=rope_tabs is not None, has_s0=s0 is not None, hp=hp),
        grid=(n_batch, GLA_HEADS // hp),
        in_specs=in_specs,
        out_specs=[
            pl.BlockSpec((t_len, vw), lambda b, h: (b, h)),
            pl.BlockSpec((None, 2, hp, GLA_DK, GLA_DV), lambda b, h: (b, 0, h, 0, 0)),
        ],
        out_shape=[
            jax.ShapeDtypeStruct((n_batch * t_len, GLA_HEADS * GLA_DV), BF16),
            jax.ShapeDtypeStruct((n_batch, 2, GLA_HEADS, GLA_DK, GLA_DV), F32),
        ],
        scratch_shapes=[pltpu.VMEM((t_len, kw), F32)] * 4 + [pltpu.VMEM((t_len, vw), F32)] * 2,
        compiler_params=_cparams(2, 48 * 2**20),
        name=name,
    )(*args)


def _merge_kernel(ya_ref, ybc_ref, ybl_ref, ycc_ref, ycl_ref, wa_ref, wb_ref, wc_ref, ga_ref, gb_ref, gc_ref, o_ref,
                  *, n_ctx_tiles):
    is_ctx = pl.program_id(0) < n_ctx_tiles
    yb = jnp.where(is_ctx, ybc_ref[...], ybl_ref[...])
    yc = jnp.where(is_ctx, ycc_ref[...], ycl_ref[...])
    m = jax.nn.sigmoid(ga_ref[...].astype(F32)) * _dot(ya_ref[...], wa_ref[...])
    m += jax.nn.sigmoid(gb_ref[...].astype(F32)) * _dot(yb, wb_ref[...])
    m += jax.nn.sigmoid(gc_ref[...].astype(F32)) * _dot(yc, wc_ref[...])
    o_ref[...] = m.astype(BF16)


def _merge(ya, yb_c, yb_l, yc_c, yc_l, wa, wb, wc, z, layer, tm=512, tn=1024):
    m, kw = ya.shape
    d = wa.shape[-1]
    nc = yb_c.shape[0] // tm
    nl = yb_l.shape[0] // tm
    assert nc * tm == yb_c.shape[0] and nl * tm == yb_l.shape[0] and nc + nl == m // tm
    y_spec = pl.BlockSpec((tm, kw), lambda i, j: (i, 0))
    ctx_spec = pl.BlockSpec((tm, kw), lambda i, j: (jnp.minimum(i, nc - 1), 0))
    lat_spec = pl.BlockSpec((tm, kw), lambda i, j: (jnp.maximum(i - nc, 0), 0))
    w_spec = pl.BlockSpec((None, kw, tn), lambda i, j: (layer, 0, j))

    def gate_spec(col):
        return pl.BlockSpec((tm, tn), lambda i, j: (i, col // tn + j))

    return pl.pallas_call(
        functools.partial(_merge_kernel, n_ctx_tiles=nc),
        grid=(m // tm, d // tn),
        in_specs=[y_spec, ctx_spec, lat_spec, ctx_spec, lat_spec, w_spec, w_spec, w_spec,
                  gate_spec(COL_GA), gate_spec(COL_GB), gate_spec(COL_GC)],
        out_specs=pl.BlockSpec((tm, tn), lambda i, j: (i, j)),
        out_shape=jax.ShapeDtypeStruct((m, d), BF16),
        compiler_params=_cparams(2, 48 * 2**20),
        name="merge",
    )(ya, yb_c, yb_l, yc_c, yc_l, wa, wb, wc, z, z, z)


def _extract_top(x, count, out_ref, with_rank=False):
    rank = jnp.full(x.shape, float(PEER_TOPK), F32)
    for j in range(count):
        m = jnp.max(x, axis=0, keepdims=True)
        out_ref[j:j + 1, :] = m
        hit = x >= m
        if with_rank and j < PEER_TOPK:
            rank = jnp.where(hit, float(j), rank)
        if j + 1 < count:
            x = jnp.where(hit, NEG, x)
    return rank if with_rank else None


def _route_kernel(q_ref, k1_ref, k2_ref, cnt_ref, r2_ref, e2_ref, c1_ref, t1, t2, c):
    nk = PEER_NKEYS
    k = PEER_TOPK

    def scores(keys, qt):
        kh, kl = _split_bf16(keys)
        qh, ql = _split_bf16(qt)
        return _dot(kh, qh) + (_dot(kh, ql) + _dot(kl, qh))

    s1 = scores(k1_ref[...], q_ref[0:nk, :])
    s2 = scores(k2_ref[...], q_ref[nk:2 * nk, :])
    _extract_top(s1, k + 1, t1)
    rank2 = _extract_top(s2, k + 1, t2, with_rank=True)
    n = s1.shape[1]
    rid8 = lax.broadcasted_iota(jnp.int32, (8, n), 0)
    extra = jnp.where(rid8 == 0, t1[k:k + 1, :] + t2[0:1, :],
                      jnp.where(rid8 == 1, t1[0:1, :] + t2[k:k + 1, :], NEG))
    cand = jnp.concatenate(
        [t1[0:1, :] + t2[0:k, :]] + [t1[a:a + 1, :] + t2[0:8, :] for a in range(1, k)] + [extra], axis=0)
    _extract_top(cand, k + 1, c)
    tau = 0.5 * (c[k - 1:k, :] + c[k:k + 1, :])
    zsum = jnp.sum(jnp.exp(c[0:k, :] - c[0:1, :]), axis=0, keepdims=True)
    th = tau - s1
    cnt = jnp.zeros_like(th)
    for b in range(k):
        cnt = jnp.where(th <= t2[b:b + 1, :], float(b + 1), cnt)
    cnt_ref[...] = cnt
    r2_ref[...] = rank2.astype(BF16)
    e2_ref[...] = jnp.exp(s2 - t2[0:1, :]).astype(BF16)
    c1_ref[...] = jnp.exp(s1 - t1[0:1, :]) / zsum


def _peer_route(qt, k1, k2, nt=512):
    m = qt.shape[1]
    nk = PEER_NKEYS
    out_f32 = jax.ShapeDtypeStruct((PEER_HEADS, nk, m), F32)
    out_b16 = jax.ShapeDtypeStruct((PEER_HEADS, nk, m), BF16)
    ospec = pl.BlockSpec((None, nk, nt), lambda i, h: (h, 0, i))
    kspec = pl.BlockSpec((nk, nk), lambda i, h: (0, 0))
    return pl.pallas_call(
        _route_kernel,
        grid=(m // nt, PEER_HEADS),
        in_specs=[pl.BlockSpec((2 * nk, nt), lambda i, h: (h, i)), kspec, kspec],
        out_specs=[ospec] * 4,
        out_shape=[out_f32, out_b16, out_b16, out_f32],
        scratch_shapes=[pltpu.VMEM((24, nt), F32)] * 3,
        compiler_params=_cparams(2, 24 * 2**20),
        name="peer_route",
    )(qt, k1, k2)


PEER_TE = 1024
PEER_RBLK = 4


def _peer_kernel(ht_ref, cnt_ref, r2_in_ref, e2_in_ref, c1_ref, u_ref, vt_ref, o_ref,
                 act_ref, g_ref, r2_ref, e2_ref, *, te, nt):
    j = pl.program_id(1)
    nk = PEER_NKEYS
    groups = te // nk

    @pl.when(j == 0)
    def _():
        o_ref[...] = jnp.zeros_like(o_ref)
        r2_ref[...] = r2_in_ref[...]
        e2_ref[...] = e2_in_ref[...]

    act_ref[...] = _dot(u_ref[...], ht_ref[...]).astype(BF16)

    def row_bf16(ref, h, r, lanes):
        row = jnp.broadcast_to(ref[h, r:r + 1, lanes], (16, 128)).astype(BF16)
        return pltpu.repeat(row, nk // 16, axis=0)

    for lb in range(nt // 128):
        lanes = slice(lb * 128, (lb + 1) * 128)
        for rb in range(groups // PEER_RBLK):
            ws = [jnp.zeros((nk, 128), BF16) for _ in range(PEER_RBLK)]
            for h in range(PEER_HEADS):
                r2t = r2_ref[h * nk:(h + 1) * nk, lanes]
                e2t = e2_ref[h * nk:(h + 1) * nk, lanes]
                for rr in range(PEER_RBLK):
                    r = rb * PEER_RBLK + rr
                    hit = r2t < row_bf16(cnt_ref, h, r, lanes)
                    ws[rr] = ws[rr] + jnp.where(hit, e2t * row_bf16(c1_ref, h, r, lanes), 0.0)
            for rr in range(PEER_RBLK):
                rows = slice((rb * PEER_RBLK + rr) * nk, (rb * PEER_RBLK + rr + 1) * nk)
                g_ref[rows, lanes] = jax.nn.gelu(act_ref[rows, lanes]) * ws[rr]

    o_ref[...] += _dot(vt_ref[...], g_ref[...])


def _peer(ht, cnt, r2, e2, c1, u, vt, layer, nt=512):
    d, m = ht.shape
    te = PEER_TE
    nk = PEER_NKEYS
    groups = te // nk
    assert groups == 8, "first-key rows of a tile must fill one sublane group"
    rspec = pl.BlockSpec((PEER_HEADS * nk, nt), lambda i, j: (0, i))
    gspec = pl.BlockSpec((PEER_HEADS, groups, nt), lambda i, j: (0, j, i))
    est = (2 * (d * nt * 2 + 2 * PEER_HEADS * nk * nt * 2 + 2 * te * d * 2 + d * nt * 4)
           + 2 * te * nt * 2 + te * nt * 4 + d * nt * 4)
    return pl.pallas_call(
        functools.partial(_peer_kernel, te=te, nt=nt),
        grid=(m // nt, N_EXPERTS // te),
        in_specs=[
            pl.BlockSpec((d, nt), lambda i, j: (0, i)),
            gspec, rspec, rspec, gspec,
            pl.BlockSpec((None, te, d), lambda i, j: (layer, j, 0)),
            pl.BlockSpec((None, None, d, te), lambda i, j: (layer, j, 0, 0)),
        ],
        out_specs=pl.BlockSpec((d, nt), lambda i, j: (0, i)),
        out_shape=jax.ShapeDtypeStruct((d, m), F32),
        scratch_shapes=[pltpu.VMEM((te, nt), BF16), pltpu.VMEM((te, nt), BF16),
                        pltpu.VMEM((PEER_HEADS * nk, nt), BF16), pltpu.VMEM((PEER_HEADS * nk, nt), BF16)],
        compiler_params=_cparams(2, est + 2 * PEER_HEADS * nk * nt * 2 + 6 * 2**20),
        name="peer_experts",
    )(ht, cnt, r2.reshape(PEER_HEADS * nk, m), e2.reshape(PEER_HEADS * nk, m), c1, u, vt)


def kernel(x_prompt, x_sample, cache_k, cache_v, state_gla, c, c_ctx, w_ada, b_ada, w_in, a_ln_g, a_ln_b, a_ws, a_bs, na_rpb, gla_wg_f, gla_bg_f, gla_wg_b, gla_bg_b, gla_norm_g, w_br_a, w_br_b, w_br_c, w_out, ln1_g, ln1_b, ln2_g, ln2_b, peer_wq, peer_k1, peer_k2, peer_u, peer_v):
    n_ctx, n_lat = x_prompt.shape[0], x_sample.shape[0]
    n_layers = w_in.shape[0]
    d = D_MODEL
    mc, ml = n_ctx * SEQ, n_lat * DEC_SEQ
    assert mc % DEC_SEQ == 0, "latent row blocks must start on a DEC_SEQ boundary of the shared token axis"
    assert n_lat + 1 <= 16

    cc = jnp.zeros((16, d), F32).at[0].set(c_ctx).at[1:1 + n_lat].set(c)
    mod = _ada(cc, w_ada, b_ada).reshape(n_layers, 16, N_MOD, 1, d)

    def mod_vec(layer, which):
        return mod[layer, :, which]

    w_main = jnp.concatenate([w_in[:, :, :LR_START], w_in[:, :, LR_END:]], axis=-1).astype(BF16)
    w_lr = jnp.pad(w_in[:, :, LR_START:LR_END], ((0, 0), (0, 0), (0, 128 - (LR_END - LR_START)))).astype(BF16)
    wgf = jnp.pad(gla_wg_f, ((0, 0), (0, 128 - GLA_RANK), (0, 0)))
    wgb = jnp.pad(gla_wg_b, ((0, 0), (GLA_RANK, 128 - 2 * GLA_RANK), (0, 0)))
    wbr_a, wbr_b, wbr_c = w_br_a.astype(BF16), w_br_b.astype(BF16), w_br_c.astype(BF16)
    w_o = w_out.astype(BF16)
    wq_t = jnp.swapaxes(peer_wq, 1, 2).astype(BF16)
    u_b = peer_u.astype(BF16)
    v_t = jnp.swapaxes(peer_v.reshape(n_layers, N_EXPERTS // PEER_TE, PEER_TE, d), 2, 3).astype(BF16)
    rope_tabs = _rope_tables(DEC_SEQ)
    bias_tab = _na_bias_tables(na_rpb)

    x = jnp.concatenate([x_prompt.reshape(mc, d), x_sample.reshape(ml, d)], axis=0)
    h = _modulate(x, mod_vec(0, 0), mod_vec(0, 1), mc)

    new_k, new_v, new_s = [], [], []
    for l in range(n_layers):
        z = _matmul(h, w_main, BF16, 512, 2048, "in_proj", b_resident=True, layer=l)
        lr = _matmul(h, w_lr[l], F32, 512, 128, "lr_proj")
        ya = _mixer_a(z, a_ln_g[l], a_ln_b[l], a_ws[l], a_bs[l])
        yb_c, k_l, v_l = _attn_ctx(z, n_ctx)
        yb_l = _attn_lat(z, cache_k, cache_v, bias_tab, l, n_lat, mc // DEC_SEQ)
        gla_args = (wgf[l], wgb[l], gla_bg_f[l], gla_bg_b[l], gla_norm_g[l])
        yc_c, s_l = _gla(z, lr, *gla_args, n_ctx, SEQ, 0, "gla_ctx", 4)
        yc_l, _ = _gla(z, lr, *gla_args, n_lat, DEC_SEQ, mc // DEC_SEQ, "gla_lat", 2,
                       rope_tabs=rope_tabs, s0=state_gla, layer=l)
        new_k.append(k_l)
        new_v.append(v_l)
        new_s.append(s_l)
        mrg = _merge(ya, yb_c, yb_l, yc_c, yc_l, wbr_a, wbr_b, wbr_c, z, l)
        y = _matmul(mrg, w_o, F32, 512, 1024, "out_proj", layer=l)
        x, h2t = _res_ln(x, y, mod_vec(l, 2), ln1_g[l], ln1_b[l], mod_vec(l, 3), mod_vec(l, 4), mc, "res_ln1",
                         h_transposed=True)
        qt = _matmul(wq_t, h2t, F32, 512, 1024, "peer_q", layer=l)
        cnt, r2, e2, c1 = _peer_route(qt, peer_k1[l], peer_k2[l])
        pt = _peer(h2t, cnt, r2, e2, c1, u_b, v_t, l)
        ln2_args = (x, pt, mod_vec(l, 5), ln2_g[l], ln2_b[l])
        if l + 1 < n_layers:
            x, h = _res_ln(*ln2_args, mod_vec(l + 1, 0), mod_vec(l + 1, 1), mc, "res_ln2", y_transposed=True)
        else:
            x_ctx, _ = _res_ln(*ln2_args, mod_vec(l, 0), mod_vec(l, 1), mc, "res_ln2_ctx", y_transposed=True,
                               rows=(0, mc))
            x_lat, _ = _res_ln(*ln2_args, mod_vec(l, 0), mod_vec(l, 1), mc, "res_ln2_lat", y_transposed=True,
                               rows=(mc, ml))

    y_prompt = x_ctx.reshape(n_ctx, SEQ, d)
    y_sample = x_lat.reshape(n_lat, DEC_SEQ, d)
    return (y_prompt, y_sample, jnp.stack(new_k, axis=1), jnp.stack(new_v, axis=1), jnp.stack(new_s, axis=1))
```

```python
import functools

import numpy as np
import jax
import jax.numpy as jnp
from jax import lax
from jax.experimental import pallas as pl
from jax.experimental.pallas import tpu as pltpu

F32 = jnp.float32
BF16 = jnp.bfloat16

D_MODEL = 2048
SEQ = 256
DEC_SEQ = 2048
GRID_W = 64
CHUNK = 128
A_GROUPS = 8
A_WIDTH = 1024
NA_HEADS = 8
NA_DH = 128
NA_WIN_H = 8
NA_WIN_W = 16
GLA_HEADS = 4
GLA_DK = 128
GLA_DV = 256
GLA_RANK = 16
GLA_GATE_TEMP = 16.0
GLA_CHUNK = 64
ROPE_BASE = 10000.0
PEER_HEADS = 8
PEER_NKEYS = 128
PEER_TOPK = 16
N_EXPERTS = PEER_NKEYS * PEER_NKEYS
N_MOD = 6
ALPHA = 8.0 ** 0.25
EPS = 1e-5
NEG = -1e30

Z_WIDTH = 14336
COL_A = 0
COL_QB, COL_KB, COL_VB = 2048, 3072, 4096
COL_QC, COL_KC, COL_VC, COL_RC = 5120, 5632, 6144, 7168
COL_GA, COL_GB, COL_GC = 8192, 10240, 12288
LR_START, LR_END = 8192, 8224

VMEM_CAP = 56 * 1024 * 1024


def _cparams(n_axes, vmem_bytes, flags=None):
    return pltpu.CompilerParams(
        dimension_semantics=("arbitrary",) * n_axes,
        vmem_limit_bytes=min(int(vmem_bytes), VMEM_CAP),
        flags=flags,
    )


def _mod_row(i, tm, n_ctx_rows):
    start = i * tm
    return jnp.where(start < n_ctx_rows, 0, 1 + (start - n_ctx_rows) // DEC_SEQ)


def _dot(a, b):
    return jnp.dot(a, b, preferred_element_type=F32)


def _dot_nt(a, b):
    return lax.dot_general(a, b, (((1,), (1,)), ((), ())), preferred_element_type=F32)


def _dot_tn(a, b):
    return lax.dot_general(a, b, (((0,), (0,)), ((), ())), preferred_element_type=F32)


def _split_bf16(x):
    hi = x.astype(BF16)
    lo = (x - hi.astype(F32)).astype(BF16)
    return hi, lo


def _ada_kernel(c_ref, w_ref, b_ref, o_ref):
    c = c_ref[...]
    a = (c * jax.nn.sigmoid(c)).astype(BF16)
    o_ref[...] = _dot(a, w_ref[...].astype(BF16)) + b_ref[...]


def _ada(cc, w_ada, b_ada):
    n_layers, d, n = w_ada.shape
    rows = cc.shape[0]
    tn = 1024
    return pl.pallas_call(
        _ada_kernel,
        grid=(n_layers, n // tn),
        in_specs=[
            pl.BlockSpec((rows, d), lambda l, j: (0, 0)),
            pl.BlockSpec((None, d, tn), lambda l, j: (l, 0, j)),
            pl.BlockSpec((None, 1, tn), lambda l, j: (l, 0, j)),
        ],
        out_specs=pl.BlockSpec((None, rows, tn), lambda l, j: (l, 0, j)),
        out_shape=jax.ShapeDtypeStruct((n_layers, rows, n), F32),
        compiler_params=_cparams(2, 2 * d * tn * 4 + 12 * 2**20),
        name="ada_mod",
    )(cc, w_ada, b_ada.reshape(n_layers, 1, n))


def _mm_kernel(a_ref, b_ref, o_ref):
    o_ref[...] = _dot(a_ref[...].astype(BF16), b_ref[...].astype(BF16)).astype(o_ref.dtype)


def _matmul(a, b, out_dtype, tm, tn, name, b_resident=False, layer=None):
    m, k = a.shape[-2:]
    n = b.shape[-1]
    assert m % tm == 0 and n % tn == 0, (m, n, tm, tn)
    est = 2 * (tm * k * a.dtype.itemsize + k * tn * b.dtype.itemsize + tm * tn * 4) + tm * tn * 8
    if b_resident:
        grid = (n // tn, m // tm)
        a_idx, b_idx, o_map = (lambda j, i: (i, 0)), (lambda j, i: (0, j)), (lambda j, i: (i, j))
    else:
        grid = (m // tm, n // tn)
        a_idx, b_idx, o_map = (lambda i, j: (i, 0)), (lambda i, j: (0, j)), (lambda i, j: (i, j))

    def spec(x, block, idx):
        if x.ndim == 2:
            return pl.BlockSpec(block, idx)
        return pl.BlockSpec((None,) + block, lambda *g: (layer,) + idx(*g))

    return pl.pallas_call(
        _mm_kernel,
        grid=grid,
        in_specs=[spec(a, (tm, k), a_idx), spec(b, (k, tn), b_idx)],
        out_specs=pl.BlockSpec((tm, tn), o_map),
        out_shape=jax.ShapeDtypeStruct((m, n), out_dtype),
        compiler_params=_cparams(2, est + 8 * 2**20),
        name=name,
    )(a, b)


def _modulate_kernel(x_ref, sh_ref, sc_ref, h_ref):
    h_ref[...] = (x_ref[...] * (1.0 + sc_ref[...]) + sh_ref[...]).astype(BF16)


def _modulate(x, sh, sc, n_ctx_rows, tm=256):
    m, d = x.shape
    vec = pl.BlockSpec((None, 1, d), lambda i: (_mod_row(i, tm, n_ctx_rows), 0, 0))
    return pl.pallas_call(
        _modulate_kernel,
        grid=(m // tm,),
        in_specs=[pl.BlockSpec((tm, d), lambda i: (i, 0)), vec, vec],
        out_specs=pl.BlockSpec((tm, d), lambda i: (i, 0)),
        out_shape=jax.ShapeDtypeStruct((m, d), BF16),
        compiler_params=_cparams(1, 32 * 2**20),
        name="modulate",
    )(x, sh, sc)


def _res_ln_kernel(x_ref, y_ref, g_ref, lg_ref, lb_ref, sh_ref, sc_ref, *rest, y_transposed, h_transposed, project):
    if project:
        w_ref, xo_ref, h_ref = rest
        y = _dot(y_ref[...], w_ref[...])
    else:
        xo_ref, h_ref = rest
        y = y_ref[...].astype(F32)
    if y_transposed:
        y = y.T
    t = ALPHA * x_ref[...] + g_ref[...] * y
    mu = jnp.mean(t, axis=-1, keepdims=True)
    tc = t - mu
    var = jnp.mean(tc * tc, axis=-1, keepdims=True)
    xn = tc * lax.rsqrt(var + EPS) * lg_ref[...] + lb_ref[...]
    xo_ref[...] = xn
    h = xn * (1.0 + sc_ref[...]) + sh_ref[...]
    h_ref[...] = (h.T if h_transposed else h).astype(BF16)


def _res_ln(x, y, g, ln_g, ln_b, sh, sc, n_ctx_rows, name, tm=256, y_transposed=False, h_transposed=False,
            rows=None, w=None, layer=None):
    d = x.shape[1]
    start, m = (0, x.shape[0]) if rows is None else rows
    t0 = start // tm
    assert start % tm == 0 and m % tm == 0
    vec = pl.BlockSpec((None, 1, d), lambda i: (_mod_row(i + t0, tm, n_ctx_rows), 0, 0))
    par = pl.BlockSpec((1, d), lambda i: (0, 0))
    row_in = pl.BlockSpec((tm, d), lambda i: (i + t0, 0))
    col_in = pl.BlockSpec((d, tm), lambda i: (0, i + t0))
    row = pl.BlockSpec((tm, d), lambda i: (i, 0))
    col = pl.BlockSpec((d, tm), lambda i: (0, i))
    in_specs = [row_in, col_in if y_transposed else row_in, vec, par, par, vec, vec]
    args = [x, y, g, ln_g.reshape(1, d), ln_b.reshape(1, d), sh, sc]
    if w is not None:
        in_specs.append(pl.BlockSpec((None, d, d), lambda i: (layer, 0, 0)))
        args.append(w)
    return pl.pallas_call(
        functools.partial(_res_ln_kernel, y_transposed=y_transposed, h_transposed=h_transposed,
                          project=w is not None),
        grid=(m // tm,),
        in_specs=in_specs,
        out_specs=[row, col if h_transposed else row],
        out_shape=[jax.ShapeDtypeStruct((m, d), F32),
                   jax.ShapeDtypeStruct((d, m) if h_transposed else (m, d), BF16)],
        compiler_params=_cparams(1, 48 * 2**20),
        name=name,
    )(*args)


def _mixer_a_kernel(a_ref, lg_ref, lb_ref, ws_ref, bs_ref, y_ref, *, n_chunks):
    gw = A_WIDTH // A_GROUPS
    for c in range(n_chunks):
        rows = slice(c * CHUNK, (c + 1) * CHUNK)
        g = jax.nn.gelu(a_ref[rows, :].astype(F32))
        u = g[:, :A_WIDTH]
        v = g[:, A_WIDTH:]
        mu = jnp.mean(v, axis=-1, keepdims=True)
        vc = v - mu
        var = jnp.mean(vc * vc, axis=-1, keepdims=True)
        v = vc * lax.rsqrt(var + EPS) * lg_ref[...] + lb_ref[...]
        for gi in range(A_GROUPS):
            cols = slice(gi * gw, (gi + 1) * gw)
            sp = _dot(ws_ref[gi].astype(BF16), v[:, cols].astype(BF16)) + bs_ref[gi]
            y_ref[rows, cols] = (u[:, cols] * sp).astype(BF16)


def _mixer_a(z, ln_g, ln_b, ws, bs, tm=256):
    m = z.shape[0]
    gw = A_WIDTH // A_GROUPS
    bs_b = jnp.broadcast_to(bs[:, :, None], (A_GROUPS, CHUNK, gw))
    return pl.pallas_call(
        functools.partial(_mixer_a_kernel, n_chunks=tm // CHUNK),
        grid=(m // tm,),
        in_specs=[
            pl.BlockSpec((tm, 2 * A_WIDTH), lambda i: (i, COL_A // (2 * A_WIDTH))),
            pl.BlockSpec((1, A_WIDTH), lambda i: (0, 0)),
            pl.BlockSpec((1, A_WIDTH), lambda i: (0, 0)),
            pl.BlockSpec((A_GROUPS, CHUNK, CHUNK), lambda i: (0, 0, 0)),
            pl.BlockSpec((A_GROUPS, CHUNK, gw), lambda i: (0, 0, 0)),
        ],
        out_specs=pl.BlockSpec((tm, A_WIDTH), lambda i: (i, 0)),
        out_shape=jax.ShapeDtypeStruct((m, A_WIDTH), BF16),
        compiler_params=_cparams(1, 32 * 2**20),
        name="mixer_a",
    )(z, ln_g.reshape(1, A_WIDTH), ln_b.reshape(1, A_WIDTH), ws, bs_b)


def _attn_ctx_kernel(q_ref, k_ref, v_ref, y_ref, ko_ref, vo_ref):
    for h in range(NA_HEADS):
        cols = slice(h * NA_DH, (h + 1) * NA_DH)
        q, k, v = q_ref[:, cols], k_ref[:, cols], v_ref[:, cols]
        s = _dot_nt(q, k) * (NA_DH ** -0.5)
        m = jnp.max(s, axis=-1, keepdims=True)
        p = jnp.exp(s - m)
        l = jnp.sum(p, axis=-1, keepdims=True)
        o = _dot(p.astype(BF16), v) / l
        y_ref[:, cols] = o.astype(BF16)
        ko_ref[h] = k.astype(F32)
        vo_ref[h] = v.astype(F32)


def _attn_ctx(z, n_ctx):
    width = NA_HEADS * NA_DH
    assert COL_QB % width == 0 and COL_KB % width == 0 and COL_VB % width == 0
    cache = jax.ShapeDtypeStruct((n_ctx, NA_HEADS, SEQ, NA_DH), F32)
    cache_spec = pl.BlockSpec((None, NA_HEADS, SEQ, NA_DH), lambda b: (b, 0, 0, 0))
    return pl.pallas_call(
        _attn_ctx_kernel,
        grid=(n_ctx,),
        in_specs=[
            pl.BlockSpec((SEQ, width), lambda b: (b, COL_QB // width)),
            pl.BlockSpec((SEQ, width), lambda b: (b, COL_KB // width)),
            pl.BlockSpec((SEQ, width), lambda b: (b, COL_VB // width)),
        ],
        out_specs=[pl.BlockSpec((SEQ, width), lambda b: (b, 0)), cache_spec, cache_spec],
        out_shape=[jax.ShapeDtypeStruct((n_ctx * SEQ, width), BF16), cache, cache],
        compiler_params=_cparams(1, 24 * 2**20),
        name="attn_ctx",
    )(z, z, z)


NA_QROWS = 4
NA_KROWS = 12
NA_NQB = (DEC_SEQ // GRID_W) // NA_QROWS


def _na_block(qb):
    n_rows = DEC_SEQ // GRID_W
    k0 = min(max(NA_QROWS * qb - NA_WIN_H // 2, 0), n_rows - NA_KROWS)
    case = 0 if qb == 0 else (2 if qb == NA_NQB - 1 else 1)
    return k0, case


def _rpb_expand_kernel(rp_ref, oh_ref, neg_ref, o_ref):
    r = rp_ref[...]
    hi = r.astype(BF16)
    r1 = r - hi.astype(F32)
    mid = r1.astype(BF16)
    lo = (r1 - mid.astype(F32)).astype(BF16)
    oh = oh_ref[...]
    o_ref[...] = (_dot(hi, oh) + _dot(mid, oh)) + (_dot(lo, oh) + neg_ref[...])


def _na_bias_tables(rpb):
    n_layers, n_heads, n_dr, n_dc = rpb.shape
    cols = np.arange(GRID_W)
    cs = np.clip(cols - NA_WIN_W // 2, 0, GRID_W - NA_WIN_W)
    ok_c = (cols[None, :] >= cs[:, None]) & (cols[None, :] < cs[:, None] + NA_WIN_W)
    dc = np.clip(cols[None, :] - cols[:, None], -(NA_WIN_W - 1), NA_WIN_W - 1) + NA_WIN_W - 1
    onehot = (np.arange(32)[:, None] == dc.reshape(1, -1)).astype(np.float32)
    negmask = np.where(ok_c.reshape(1, -1), 0.0, NEG).astype(np.float32)
    rows = n_layers * n_heads * n_dr
    rows_pad = -(-rows // 128) * 128
    rp = jnp.pad(rpb.reshape(rows, n_dc), ((0, rows_pad - rows), (0, 32 - n_dc)))
    t = pl.pallas_call(
        _rpb_expand_kernel,
        grid=(rows_pad // 128,),
        in_specs=[pl.BlockSpec((128, 32), lambda i: (i, 0)),
                  pl.BlockSpec((32, GRID_W * GRID_W), lambda i: (0, 0)),
                  pl.BlockSpec((1, GRID_W * GRID_W), lambda i: (0, 0))],
        out_specs=pl.BlockSpec((128, GRID_W * GRID_W), lambda i: (i, 0)),
        out_shape=jax.ShapeDtypeStruct((rows_pad, GRID_W * GRID_W), F32),
        compiler_params=_cparams(1, 16 * 2**20),
        name="rpb_expand",
    )(rp, jnp.asarray(onehot, BF16), jnp.asarray(negmask))
    t = t[:rows].reshape(n_layers, n_heads, n_dr, GRID_W, GRID_W)
    n_rows = DEC_SEQ // GRID_W
    neg_blk = jnp.full((n_layers, n_heads, GRID_W, GRID_W), NEG, F32)
    cases = []
    for qb in (0, 1, NA_NQB - 1):
        k0, _ = _na_block(qb)
        q_rows = []
        for qr in range(NA_QROWS):
            r = NA_QROWS * qb + qr
            rs = min(max(r - NA_WIN_H // 2, 0), n_rows - NA_WIN_H)
            blks = []
            for j in range(NA_KROWS):
                kr = k0 + j
                blks.append(t[:, :, kr - r + NA_WIN_H - 1] if rs <= kr < rs + NA_WIN_H else neg_blk)
            q_rows.append(jnp.concatenate(blks, axis=-1))
        cases.append(jnp.concatenate(q_rows, axis=-2))
    return jnp.stack(cases, axis=2)


def _attn_lat_kernel(q_ref, k_ref, v_ref, ck_ref, cv_ref, bias_ref, y_ref):
    scale = NA_DH ** -0.5
    ck = ck_ref[...].astype(BF16)
    cv = cv_ref[...].astype(BF16)
    nq = NA_QROWS * GRID_W
    nk = NA_KROWS * GRID_W
    for qb in range(NA_NQB):
        k0, case = _na_block(qb)
        q = q_ref[qb * nq:(qb + 1) * nq, :]
        kw = k_ref[k0 * GRID_W:k0 * GRID_W + nk, :]
        vw = v_ref[k0 * GRID_W:k0 * GRID_W + nk, :]
        sw = _dot_nt(q, kw) * scale + bias_ref[case]
        sc = _dot_nt(q, ck) * scale
        m = jnp.maximum(jnp.max(sw, axis=-1, keepdims=True), jnp.max(sc, axis=-1, keepdims=True))
        pw = jnp.exp(sw - m)
        pc = jnp.exp(sc - m)
        l = jnp.sum(pw, axis=-1, keepdims=True) + jnp.sum(pc, axis=-1, keepdims=True)
        o = _dot(pw.astype(BF16), vw) + _dot(pc.astype(BF16), cv)
        y_ref[qb * nq:(qb + 1) * nq, :] = (o / l).astype(BF16)


def _attn_lat(z, cache_k, cache_v, bias_tab, layer, n_lat, row_blk0):
    qb, kb, vb = COL_QB // NA_DH, COL_KB // NA_DH, COL_VB // NA_DH
    past = cache_k.shape[3]
    cspec = pl.BlockSpec((None, None, None, past, NA_DH), lambda b, h: (b, layer, h, 0, 0))
    return pl.pallas_call(
        _attn_lat_kernel,
        grid=(n_lat, NA_HEADS),
        in_specs=[
            pl.BlockSpec((DEC_SEQ, NA_DH), lambda b, h: (row_blk0 + b, qb + h)),
            pl.BlockSpec((DEC_SEQ, NA_DH), lambda b, h: (row_blk0 + b, kb + h)),
            pl.BlockSpec((DEC_SEQ, NA_DH), lambda b, h: (row_blk0 + b, vb + h)),
            cspec, cspec,
            pl.BlockSpec((None, None, 3, NA_QROWS * GRID_W, NA_KROWS * GRID_W), lambda b, h: (layer, h, 0, 0, 0)),
        ],
        out_specs=pl.BlockSpec((DEC_SEQ, NA_DH), lambda b, h: (b, h)),
        out_shape=jax.ShapeDtypeStruct((n_lat * DEC_SEQ, NA_HEADS * NA_DH), BF16),
        compiler_params=_cparams(2, 40 * 2**20),
        name="attn_lat",
    )(z, z, z, cache_k, cache_v, bias_tab)


def _rope_tables(t_len):
    half = GLA_DK // 2
    nf = half // 2
    t = np.arange(t_len)
    inv = ROPE_BASE ** (-np.arange(nf, dtype=np.float32) / nf)
    ang_r = (t // GRID_W).astype(np.float32)[:, None] * inv
    ang_c = (t % GRID_W).astype(np.float32)[:, None] * inv
    cos = np.concatenate([np.cos(ang_r), np.cos(ang_r), np.cos(ang_c), np.cos(ang_c)], axis=-1)
    sin = np.concatenate([-np.sin(ang_r), np.sin(ang_r), -np.sin(ang_c), np.sin(ang_c)], axis=-1)
    return jnp.asarray(cos, F32), jnp.asarray(sin, F32)


def _log_sigmoid(x):
    return jnp.minimum(x, 0.0) - jnp.log(1.0 + jnp.exp(-jnp.abs(x)))


def _gla_kernel(*refs, t_len, rope, has_s0, hp):
    refs = list(refs)
    q_ref, k_ref, v_ref, r_ref, lr_ref, wgf_ref, wgb_ref, bgf_ref, bgb_ref, ng_ref = refs[:10]
    pos = 10
    if rope:
        cos_ref, sin_ref = refs[pos:pos + 2]
        pos += 2
    if has_s0:
        s0_ref = refs[pos]
        pos += 1
    y_ref, sfin_ref, qs, ks, gfs, gbs, of, ob = refs[pos:pos + 8]

    gc = GLA_CHUNK
    n_chunks = t_len // gc
    inv_temp = 1.0 / GLA_GATE_TEMP
    lrb = lr_ref[...].astype(BF16)
    if rope:
        lane = lax.broadcasted_iota(jnp.int32, (t_len, GLA_DK), 1)
        first = (lane % (GLA_DK // 2)) < (GLA_DK // 4)

        def swap(x):
            return jnp.where(first, pltpu.roll(x, GLA_DK - GLA_DK // 4, 1), pltpu.roll(x, GLA_DK // 4, 1))

    for hh in range(hp):
        kcols = slice(hh * GLA_DK, (hh + 1) * GLA_DK)
        q = q_ref[:, kcols].astype(F32)
        k = k_ref[:, kcols].astype(F32)
        if rope:
            q = q * cos_ref[...] + swap(q) * sin_ref[...]
            k = k * cos_ref[...] + swap(k) * sin_ref[...]
        qs[:, kcols] = q * (GLA_DK ** -0.5)
        ks[:, kcols] = k
        gfs[:, kcols] = _log_sigmoid(_dot(lrb, wgf_ref[:, kcols].astype(BF16)) + bgf_ref[:, kcols]) * inv_temp
        gbs[:, kcols] = _log_sigmoid(_dot(lrb, wgb_ref[:, kcols].astype(BF16)) + bgb_ref[:, kcols]) * inv_temp

    row = lax.broadcasted_iota(jnp.int32, (gc, gc), 0)
    col = lax.broadcasted_iota(jnp.int32, (gc, gc), 1)
    mid = gc // 2

    def chunk(c, hh, st, g_ref, causal):
        sl = pl.ds(pl.multiple_of(c * gc, gc), gc)
        kcols = slice(hh * GLA_DK, (hh + 1) * GLA_DK)
        vcols = slice(hh * GLA_DV, (hh + 1) * GLA_DV)
        mask = (row >= col) if causal else (row <= col)
        tri = mask.astype(BF16)
        qc, kc, vc, g = qs[sl, kcols], ks[sl, kcols], v_ref[sl, vcols], g_ref[sl, kcols]
        ghi, glo = _split_bf16(g)
        b = _dot(tri, ghi) + _dot(tri, glo)
        bmid = b[mid:mid + 1, :]
        tot = b[gc - 1:gc, :] if causal else b[0:1, :]
        q_in = qc * jnp.exp(b)
        qg = qc * jnp.exp(b - bmid)
        kg = kc * jnp.exp(bmid - b)
        kd = kc * jnp.exp(tot - b)
        sc = jnp.where(mask, _dot_nt(qg.astype(BF16), kg.astype(BF16)), 0.0)
        o = _dot(sc.astype(BF16), vc) + _dot_nt(q_in.astype(BF16), st.astype(BF16))
        st_new = jnp.exp(tot) * st + _dot_tn(vc, kd.astype(BF16))
        return o, st_new

    if has_s0:
        init = tuple(s0_ref[dr, hh].T for hh in range(hp) for dr in range(2))
    else:
        init = tuple(jnp.zeros((GLA_DV, GLA_DK), F32) for _ in range(2 * hp))

    def scan_step(i, carry):
        cb = n_chunks - 1 - i
        out = []
        for hh in range(hp):
            vcols = slice(hh * GLA_DV, (hh + 1) * GLA_DV)
            o_f, st_f = chunk(i, hh, carry[2 * hh], gfs, True)
            o_b, st_b = chunk(cb, hh, carry[2 * hh + 1], gbs, False)
            of[pl.ds(pl.multiple_of(i * gc, gc), gc), vcols] = o_f
            ob[pl.ds(pl.multiple_of(cb * gc, gc), gc), vcols] = o_b
            out += [st_f, st_b]
        return tuple(out)

    final = lax.fori_loop(0, n_chunks, scan_step, init)
    for hh in range(hp):
        sfin_ref[0, hh] = final[2 * hh].T
        sfin_ref[1, hh] = final[2 * hh + 1].T

    def finish(c, carry):
        sl = pl.ds(pl.multiple_of(c * gc, gc), gc)
        for hh in range(hp):
            vcols = slice(hh * GLA_DV, (hh + 1) * GLA_DV)
            o = of[sl, vcols] + ob[sl, vcols]
            o = o * lax.rsqrt(jnp.mean(o * o, axis=-1, keepdims=True) + EPS) * ng_ref[:, vcols]
            r = r_ref[sl, vcols].astype(F32)
            y_ref[sl, vcols] = (o * (r * jax.nn.sigmoid(r))).astype(BF16)
        return carry

    lax.fori_loop(0, n_chunks, finish, 0)


def _gla(z, lr, wgf, wgb, bgf, bgb, norm_g, n_batch, t_len, row_blk0, name, hp, rope_tabs=None, s0=None, layer=0):
    kw, vw = hp * GLA_DK, hp * GLA_DV
    assert GLA_HEADS % hp == 0 and COL_QC % kw == 0 and COL_KC % kw == 0 and COL_VC % vw == 0 and COL_RC % vw == 0
    qc, kc, vc, rc = COL_QC // kw, COL_KC // kw, COL_VC // vw, COL_RC // vw
    kdim = GLA_HEADS * GLA_DK
    in_specs = [
        pl.BlockSpec((t_len, kw), lambda b, h: (row_blk0 + b, qc + h)),
        pl.BlockSpec((t_len, kw), lambda b, h: (row_blk0 + b, kc + h)),
        pl.BlockSpec((t_len, vw), lambda b, h: (row_blk0 + b, vc + h)),
        pl.BlockSpec((t_len, vw), lambda b, h: (row_blk0 + b, rc + h)),
        pl.BlockSpec((t_len, 128), lambda b, h: (row_blk0 + b, 0)),
        pl.BlockSpec((128, kw), lambda b, h: (0, h)),
        pl.BlockSpec((128, kw), lambda b, h: (0, h)),
        pl.BlockSpec((1, kw), lambda b, h: (0, h)),
        pl.BlockSpec((1, kw), lambda b, h: (0, h)),
        pl.BlockSpec((1, vw), lambda b, h: (0, h)),
    ]
    args = [z, z, z, z, lr, wgf, wgb, bgf.reshape(1, kdim), bgb.reshape(1, kdim),
            norm_g.reshape(1, GLA_HEADS * GLA_DV)]
    if rope_tabs is not None:
        in_specs += [pl.BlockSpec((t_len, GLA_DK), lambda b, h: (0, 0))] * 2
        args += list(rope_tabs)
    if s0 is not None:
        in_specs.append(pl.BlockSpec((None, None, 2, hp, GLA_DK, GLA_DV), lambda b, h: (b, layer, 0, h, 0, 0)))
        args.append(s0)
    return pl.pallas_call(
        functools.partial(_gla_kernel, t_len=t_len, rope=rope_tabs is not None, has_s0=s0 is not None, hp=hp),
        grid=(n_batch, GLA_HEADS // hp),
        in_specs=in_specs,
        out_specs=[
            pl.BlockSpec((t_len, vw), lambda b, h: (b, h)),
            pl.BlockSpec((None, 2, hp, GLA_DK, GLA_DV), lambda b, h: (b, 0, h, 0, 0)),
        ],
        out_shape=[
            jax.ShapeDtypeStruct((n_batch * t_len, GLA_HEADS * GLA_DV), BF16),
            jax.ShapeDtypeStruct((n_batch, 2, GLA_HEADS, GLA_DK, GLA_DV), F32),
        ],
        scratch_shapes=[pltpu.VMEM((t_len, kw), F32)] * 4 + [pltpu.VMEM((t_len, vw), F32)] * 2,
        compiler_params=_cparams(2, 48 * 2**20),
        name=name,
    )(*args)


def _merge_kernel(ya_ref, ybc_ref, ybl_ref, ycc_ref, ycl_ref, wa_ref, wb_ref, wc_ref, ga_ref, gb_ref, gc_ref, o_ref,
                  *, n_ctx_tiles):
    is_ctx = pl.program_id(0) < n_ctx_tiles
    yb = jnp.where(is_ctx, ybc_ref[...], ybl_ref[...])
    yc = jnp.where(is_ctx, ycc_ref[...], ycl_ref[...])
    m = jax.nn.sigmoid(ga_ref[...].astype(F32)) * _dot(ya_ref[...], wa_ref[...])
    m += jax.nn.sigmoid(gb_ref[...].astype(F32)) * _dot(yb, wb_ref[...])
    m += jax.nn.sigmoid(gc_ref[...].astype(F32)) * _dot(yc, wc_ref[...])
    o_ref[...] = m.astype(BF16)


def _merge(ya, yb_c, yb_l, yc_c, yc_l, wa, wb, wc, z, layer, tm=512, tn=1024):
    m, kw = ya.shape
    d = wa.shape[-1]
    nc = yb_c.shape[0] // tm
    nl = yb_l.shape[0] // tm
    assert nc * tm == yb_c.shape[0] and nl * tm == yb_l.shape[0] and nc + nl == m // tm
    y_spec = pl.BlockSpec((tm, kw), lambda i, j: (i, 0))
    ctx_spec = pl.BlockSpec((tm, kw), lambda i, j: (jnp.minimum(i, nc - 1), 0))
    lat_spec = pl.BlockSpec((tm, kw), lambda i, j: (jnp.maximum(i - nc, 0), 0))
    w_spec = pl.BlockSpec((None, kw, tn), lambda i, j: (layer, 0, j))

    def gate_spec(col):
        return pl.BlockSpec((tm, tn), lambda i, j: (i, col // tn + j))

    return pl.pallas_call(
        functools.partial(_merge_kernel, n_ctx_tiles=nc),
        grid=(m // tm, d // tn),
        in_specs=[y_spec, ctx_spec, lat_spec, ctx_spec, lat_spec, w_spec, w_spec, w_spec,
                  gate_spec(COL_GA), gate_spec(COL_GB), gate_spec(COL_GC)],
        out_specs=pl.BlockSpec((tm, tn), lambda i, j: (i, j)),
        out_shape=jax.ShapeDtypeStruct((m, d), BF16),
        compiler_params=_cparams(2, 48 * 2**20),
        name="merge",
    )(ya, yb_c, yb_l, yc_c, yc_l, wa, wb, wc, z, z, z)


def _extract_top(x, count, out_ref, with_rank=False):
    rank = jnp.full(x.shape, float(PEER_TOPK), F32)
    for j in range(count):
        m = jnp.max(x, axis=0, keepdims=True)
        out_ref[j:j + 1, :] = m
        hit = x >= m
        if with_rank and j < PEER_TOPK:
            rank = jnp.where(hit, float(j), rank)
        if j + 1 < count:
            x = jnp.where(hit, NEG, x)
    return rank if with_rank else None


def _route_kernel(q_ref, k1_ref, k2_ref, cnt_ref, r2_ref, e2_ref, c1_ref, t1, t2, c):
    nk = PEER_NKEYS
    k = PEER_TOPK

    def scores(keys, qt):
        kh, kl = _split_bf16(keys)
        qh, ql = _split_bf16(qt)
        return _dot(kh, qh) + (_dot(kh, ql) + _dot(kl, qh))

    s1 = scores(k1_ref[...], q_ref[0:nk, :])
    s2 = scores(k2_ref[...], q_ref[nk:2 * nk, :])
    _extract_top(s1, k + 1, t1)
    rank2 = _extract_top(s2, k + 1, t2, with_rank=True)
    n = s1.shape[1]
    rid8 = lax.broadcasted_iota(jnp.int32, (8, n), 0)
    extra = jnp.where(rid8 == 0, t1[k:k + 1, :] + t2[0:1, :],
                      jnp.where(rid8 == 1, t1[0:1, :] + t2[k:k + 1, :], NEG))
    cand = jnp.concatenate(
        [t1[0:1, :] + t2[0:k, :]] + [t1[a:a + 1, :] + t2[0:8, :] for a in range(1, k)] + [extra], axis=0)
    _extract_top(cand, k + 1, c)
    tau = 0.5 * (c[k - 1:k, :] + c[k:k + 1, :])
    zsum = jnp.sum(jnp.exp(c[0:k, :] - c[0:1, :]), axis=0, keepdims=True)
    th = tau - s1
    cnt = jnp.zeros_like(th)
    for b in range(k):
        cnt = jnp.where(th <= t2[b:b + 1, :], float(b + 1), cnt)
    cnt_ref[...] = cnt
    r2_ref[...] = rank2.astype(BF16)
    e2_ref[...] = jnp.exp(s2 - t2[0:1, :]).astype(BF16)
    c1_ref[...] = jnp.exp(s1 - t1[0:1, :]) / zsum


def _peer_route(qt, k1, k2, nt=512):
    m = qt.shape[1]
    nk = PEER_NKEYS
    out_f32 = jax.ShapeDtypeStruct((PEER_HEADS, nk, m), F32)
    out_b16 = jax.ShapeDtypeStruct((PEER_HEADS, nk, m), BF16)
    ospec = pl.BlockSpec((None, nk, nt), lambda i, h: (h, 0, i))
    kspec = pl.BlockSpec((nk, nk), lambda i, h: (0, 0))
    return pl.pallas_call(
        _route_kernel,
        grid=(m // nt, PEER_HEADS),
        in_specs=[pl.BlockSpec((2 * nk, nt), lambda i, h: (h, i)), kspec, kspec],
        out_specs=[ospec] * 4,
        out_shape=[out_f32, out_b16, out_b16, out_f32],
        scratch_shapes=[pltpu.VMEM((24, nt), F32)] * 3,
        compiler_params=_cparams(2, 24 * 2**20),
        name="peer_route",
    )(qt, k1, k2)


PEER_TE = 1024
PEER_RBLK = 4


def _peer_kernel(ht_ref, cnt_ref, r2_in_ref, e2_in_ref, c1_ref, u_ref, vt_ref, o_ref,
                 act_ref, g_ref, r2_ref, e2_ref, *, te, nt):
    j = pl.program_id(1)
    nk = PEER_NKEYS
    groups = te // nk

    @pl.when(j == 0)
    def _():
        o_ref[...] = jnp.zeros_like(o_ref)
        r2_ref[...] = r2_in_ref[...]
        e2_ref[...] = e2_in_ref[...]

    act_ref[...] = _dot(u_ref[...], ht_ref[...]).astype(BF16)

    def row_bf16(ref, h, r, lanes):
        row = jnp.broadcast_to(ref[h, r:r + 1, lanes], (16, 128)).astype(BF16)
        return pltpu.repeat(row, nk // 16, axis=0)

    for lb in range(nt // 128):
        lanes = slice(lb * 128, (lb + 1) * 128)
        for rb in range(groups // PEER_RBLK):
            ws = [jnp.zeros((nk, 128), BF16) for _ in range(PEER_RBLK)]
            for h in range(PEER_HEADS):
                r2t = r2_ref[h * nk:(h + 1) * nk, lanes]
                e2t = e2_ref[h * nk:(h + 1) * nk, lanes]
                for rr in range(PEER_RBLK):
                    r = rb * PEER_RBLK + rr
                    hit = r2t < row_bf16(cnt_ref, h, r, lanes)
                    ws[rr] = ws[rr] + jnp.where(hit, e2t * row_bf16(c1_ref, h, r, lanes), 0.0)
            for rr in range(PEER_RBLK):
                rows = slice((rb * PEER_RBLK + rr) * nk, (rb * PEER_RBLK + rr + 1) * nk)
                g_ref[rows, lanes] = jax.nn.gelu(act_ref[rows, lanes]) * ws[rr]

    o_ref[...] += _dot(vt_ref[...], g_ref[...])


def _peer(ht, cnt, r2, e2, c1, u, vt, layer, nt=512):
    d, m = ht.shape
    te = PEER_TE
    nk = PEER_NKEYS
    groups = te // nk
    assert groups == 8, "first-key rows of a tile must fill one sublane group"
    rspec = pl.BlockSpec((PEER_HEADS * nk, nt), lambda i, j: (0, i))
    gspec = pl.BlockSpec((PEER_HEADS, groups, nt), lambda i, j: (0, j, i))
    est = (2 * (d * nt * 2 + 2 * PEER_HEADS * nk * nt * 2 + 2 * te * d * 2 + d * nt * 4)
           + 2 * te * nt * 2 + te * nt * 4 + d * nt * 4)
    return pl.pallas_call(
        functools.partial(_peer_kernel, te=te, nt=nt),
        grid=(m // nt, N_EXPERTS // te),
        in_specs=[
            pl.BlockSpec((d, nt), lambda i, j: (0, i)),
            gspec, rspec, rspec, gspec,
            pl.BlockSpec((None, te, d), lambda i, j: (layer, j, 0)),
            pl.BlockSpec((None, None, d, te), lambda i, j: (layer, j, 0, 0)),
        ],
        out_specs=pl.BlockSpec((d, nt), lambda i, j: (0, i)),
        out_shape=jax.ShapeDtypeStruct((d, m), F32),
        scratch_shapes=[pltpu.VMEM((te, nt), BF16), pltpu.VMEM((te, nt), BF16),
                        pltpu.VMEM((PEER_HEADS * nk, nt), BF16), pltpu.VMEM((PEER_HEADS * nk, nt), BF16)],
        compiler_params=_cparams(2, est + 2 * PEER_HEADS * nk * nt * 2 + 6 * 2**20),
        name="peer_experts",
    )(ht, cnt, r2.reshape(PEER_HEADS * nk, m), e2.reshape(PEER_HEADS * nk, m), c1, u, vt)


def kernel(x_prompt, x_sample, cache_k, cache_v, state_gla, c, c_ctx, w_ada, b_ada, w_in, a_ln_g, a_ln_b, a_ws, a_bs, na_rpb, gla_wg_f, gla_bg_f, gla_wg_b, gla_bg_b, gla_norm_g, w_br_a, w_br_b, w_br_c, w_out, ln1_g, ln1_b, ln2_g, ln2_b, peer_wq, peer_k1, peer_k2, peer_u, peer_v):
    n_ctx, n_lat = x_prompt.shape[0], x_sample.shape[0]
    n_layers = w_in.shape[0]
    d = D_MODEL
    mc, ml = n_ctx * SEQ, n_lat * DEC_SEQ
    assert mc % DEC_SEQ == 0, "latent row blocks must start on a DEC_SEQ boundary of the shared token axis"
    assert n_lat + 1 <= 16

    cc = jnp.zeros((16, d), F32).at[0].set(c_ctx).at[1:1 + n_lat].set(c)
    mod = _ada(cc, w_ada, b_ada).reshape(n_layers, 16, N_MOD, 1, d)

    def mod_vec(layer, which):
        return mod[layer, :, which]

    w_main = jnp.concatenate([w_in[:, :, :LR_START], w_in[:, :, LR_END:]], axis=-1).astype(BF16)
    w_lr = jnp.pad(w_in[:, :, LR_START:LR_END], ((0, 0), (0, 0), (0, 128 - (LR_END - LR_START)))).astype(BF16)
    wgf = jnp.pad(gla_wg_f, ((0, 0), (0, 128 - GLA_RANK), (0, 0)))
    wgb = jnp.pad(gla_wg_b, ((0, 0), (GLA_RANK, 128 - 2 * GLA_RANK), (0, 0)))
    wbr_a, wbr_b, wbr_c = w_br_a.astype(BF16), w_br_b.astype(BF16), w_br_c.astype(BF16)
    w_o = w_out.astype(BF16)
    wq_t = jnp.swapaxes(peer_wq, 1, 2).astype(BF16)
    u_b = peer_u.astype(BF16)
    v_t = jnp.swapaxes(peer_v.reshape(n_layers, N_EXPERTS // PEER_TE, PEER_TE, d), 2, 3).astype(BF16)
    rope_tabs = _rope_tables(DEC_SEQ)
    bias_tab = _na_bias_tables(na_rpb)

    x = jnp.concatenate([x_prompt.reshape(mc, d), x_sample.reshape(ml, d)], axis=0)
    h = _modulate(x, mod_vec(0, 0), mod_vec(0, 1), mc)

    new_k, new_v, new_s = [], [], []
    for l in range(n_layers):
        z = _matmul(h, w_main, BF16, 512, 2048, "in_proj", b_resident=True, layer=l)
        lr = _matmul(h, w_lr[l], F32, 512, 128, "lr_proj")
        ya = _mixer_a(z, a_ln_g[l], a_ln_b[l], a_ws[l], a_bs[l])
        yb_c, k_l, v_l = _attn_ctx(z, n_ctx)
        yb_l = _attn_lat(z, cache_k, cache_v, bias_tab, l, n_lat, mc // DEC_SEQ)
        gla_args = (wgf[l], wgb[l], gla_bg_f[l], gla_bg_b[l], gla_norm_g[l])
        yc_c, s_l = _gla(z, lr, *gla_args, n_ctx, SEQ, 0, "gla_ctx", 4)
        yc_l, _ = _gla(z, lr, *gla_args, n_lat, DEC_SEQ, mc // DEC_SEQ, "gla_lat", 2,
                       rope_tabs=rope_tabs, s0=state_gla, layer=l)
        new_k.append(k_l)
        new_v.append(v_l)
        new_s.append(s_l)
        mrg = _merge(ya, yb_c, yb_l, yc_c, yc_l, wbr_a, wbr_b, wbr_c, z, l)
        x, h2t = _res_ln(x, mrg, mod_vec(l, 2), ln1_g[l], ln1_b[l], mod_vec(l, 3), mod_vec(l, 4), mc, "out_ln1",
                         h_transposed=True, w=w_o, layer=l)
        qt = _matmul(wq_t, h2t, F32, 512, 1024, "peer_q", layer=l)
        cnt, r2, e2, c1 = _peer_route(qt, peer_k1[l], peer_k2[l])
        pt = _peer(h2t, cnt, r2, e2, c1, u_b, v_t, l)
        ln2_args = (x, pt, mod_vec(l, 5), ln2_g[l], ln2_b[l])
        if l + 1 < n_layers:
            x, h = _res_ln(*ln2_args, mod_vec(l + 1, 0), mod_vec(l + 1, 1), mc, "res_ln2", y_transposed=True)
        else:
            x_ctx, _ = _res_ln(*ln2_args, mod_vec(l, 0), mod_vec(l, 1), mc, "res_ln2_ctx", y_transposed=True,
                               rows=(0, mc))
            x_lat, _ = _res_ln(*ln2_args, mod_vec(l, 0), mod_vec(l, 1), mc, "res_ln2_lat", y_transposed=True,
                               rows=(mc, ml))

    y_prompt = x_ctx.reshape(n_ctx, SEQ, d)
    y_sample = x_lat.reshape(n_lat, DEC_SEQ, d)
    return (y_prompt, y_sample, jnp.stack(new_k, axis=1), jnp.stack(new_v, axis=1), jnp.stack(new_s, axis=1))
```

```python
import functools

import numpy as np
import jax
import jax.numpy as jnp
from jax import lax
from jax.experimental import pallas as pl
from jax.experimental.pallas import tpu as pltpu

F32 = jnp.float32
BF16 = jnp.bfloat16

D_MODEL = 2048
SEQ = 256
DEC_SEQ = 2048
GRID_W = 64
CHUNK = 128
A_GROUPS = 8
A_WIDTH = 1024
NA_HEADS = 8
NA_DH = 128
NA_WIN_H = 8
NA_WIN_W = 16
GLA_HEADS = 4
GLA_DK = 128
GLA_DV = 256
GLA_RANK = 16
GLA_GATE_TEMP = 16.0
GLA_CHUNK = 64
ROPE_BASE = 10000.0
PEER_HEADS = 8
PEER_NKEYS = 128
PEER_TOPK = 16
N_EXPERTS = PEER_NKEYS * PEER_NKEYS
N_MOD = 6
ALPHA = 8.0 ** 0.25
EPS = 1e-5
NEG = -1e30

Z_WIDTH = 14336
COL_A = 0
COL_QB, COL_KB, COL_VB = 2048, 3072, 4096
COL_QC, COL_KC, COL_VC, COL_RC = 5120, 5632, 6144, 7168
COL_GA, COL_GB, COL_GC = 8192, 10240, 12288
LR_START, LR_END = 8192, 8224

VMEM_CAP = 56 * 1024 * 1024


def _cparams(n_axes, vmem_bytes, flags=None):
    return pltpu.CompilerParams(
        dimension_semantics=("arbitrary",) * n_axes,
        vmem_limit_bytes=min(int(vmem_bytes), VMEM_CAP),
        flags=flags,
    )


def _mod_row(i, tm, n_ctx_rows):
    start = i * tm
    return jnp.where(start < n_ctx_rows, 0, 1 + (start - n_ctx_rows) // DEC_SEQ)


def _dot(a, b):
    return jnp.dot(a, b, preferred_element_type=F32)


def _dot_nt(a, b):
    return lax.dot_general(a, b, (((1,), (1,)), ((), ())), preferred_element_type=F32)


def _dot_tn(a, b):
    return lax.dot_general(a, b, (((0,), (0,)), ((), ())), preferred_element_type=F32)


def _split_bf16(x):
    hi = x.astype(BF16)
    lo = (x - hi.astype(F32)).astype(BF16)
    return hi, lo


def _ada_kernel(c_ref, w_ref, b_ref, o_ref):
    c = c_ref[...]
    a = (c * jax.nn.sigmoid(c)).astype(BF16)
    o_ref[...] = _dot(a, w_ref[...].astype(BF16)) + b_ref[...]


def _ada(cc, w_ada, b_ada):
    n_layers, d, n = w_ada.shape
    rows = cc.shape[0]
    tn = 1024
    return pl.pallas_call(
        _ada_kernel,
        grid=(n_layers, n // tn),
        in_specs=[
            pl.BlockSpec((rows, d), lambda l, j: (0, 0)),
            pl.BlockSpec((None, d, tn), lambda l, j: (l, 0, j)),
            pl.BlockSpec((None, 1, tn), lambda l, j: (l, 0, j)),
        ],
        out_specs=pl.BlockSpec((None, rows, tn), lambda l, j: (l, 0, j)),
        out_shape=jax.ShapeDtypeStruct((n_layers, rows, n), F32),
        compiler_params=_cparams(2, 2 * d * tn * 4 + 12 * 2**20),
        name="ada_mod",
    )(cc, w_ada, b_ada.reshape(n_layers, 1, n))


def _mm_kernel(a_ref, b_ref, o_ref):
    o_ref[...] = _dot(a_ref[...].astype(BF16), b_ref[...].astype(BF16)).astype(o_ref.dtype)


def _matmul(a, b, out_dtype, tm, tn, name, b_resident=False, layer=None):
    m, k = a.shape[-2:]
    n = b.shape[-1]
    assert m % tm == 0 and n % tn == 0, (m, n, tm, tn)
    est = 2 * (tm * k * a.dtype.itemsize + k * tn * b.dtype.itemsize + tm * tn * 4) + tm * tn * 8
    if b_resident:
        grid = (n // tn, m // tm)
        a_idx, b_idx, o_map = (lambda j, i: (i, 0)), (lambda j, i: (0, j)), (lambda j, i: (i, j))
    else:
        grid = (m // tm, n // tn)
        a_idx, b_idx, o_map = (lambda i, j: (i, 0)), (lambda i, j: (0, j)), (lambda i, j: (i, j))

    def spec(x, block, idx):
        if x.ndim == 2:
            return pl.BlockSpec(block, idx)
        return pl.BlockSpec((None,) + block, lambda *g: (layer,) + idx(*g))

    return pl.pallas_call(
        _mm_kernel,
        grid=grid,
        in_specs=[spec(a, (tm, k), a_idx), spec(b, (k, tn), b_idx)],
        out_specs=pl.BlockSpec((tm, tn), o_map),
        out_shape=jax.ShapeDtypeStruct((m, n), out_dtype),
        compiler_params=_cparams(2, est + 8 * 2**20),
        name=name,
    )(a, b)


def _modulate_kernel(x_ref, sh_ref, sc_ref, h_ref):
    h_ref[...] = (x_ref[...] * (1.0 + sc_ref[...]) + sh_ref[...]).astype(BF16)


def _modulate(x, sh, sc, n_ctx_rows, tm=256):
    m, d = x.shape
    vec = pl.BlockSpec((None, 1, d), lambda i: (_mod_row(i, tm, n_ctx_rows), 0, 0))
    return pl.pallas_call(
        _modulate_kernel,
        grid=(m // tm,),
        in_specs=[pl.BlockSpec((tm, d), lambda i: (i, 0)), vec, vec],
        out_specs=pl.BlockSpec((tm, d), lambda i: (i, 0)),
        out_shape=jax.ShapeDtypeStruct((m, d), BF16),
        compiler_params=_cparams(1, 32 * 2**20),
        name="modulate",
    )(x, sh, sc)


def _res_ln_kernel(x_ref, y_ref, g_ref, lg_ref, lb_ref, sh_ref, sc_ref, *rest, y_transposed, h_transposed, project):
    if project:
        w_ref, xo_ref, h_ref = rest
        y = _dot(y_ref[...], w_ref[...])
    else:
        xo_ref, h_ref = rest
        y = y_ref[...].astype(F32)
    if y_transposed:
        y = y.T
    t = ALPHA * x_ref[...] + g_ref[...] * y
    mu = jnp.mean(t, axis=-1, keepdims=True)
    tc = t - mu
    var = jnp.mean(tc * tc, axis=-1, keepdims=True)
    xn = tc * lax.rsqrt(var + EPS) * lg_ref[...] + lb_ref[...]
    xo_ref[...] = xn
    h = xn * (1.0 + sc_ref[...]) + sh_ref[...]
    h_ref[...] = (h.T if h_transposed else h).astype(BF16)


def _res_ln(x, y, g, ln_g, ln_b, sh, sc, n_ctx_rows, name, tm=256, y_transposed=False, h_transposed=False,
            rows=None, w=None, layer=None):
    d = x.shape[1]
    start, m = (0, x.shape[0]) if rows is None else rows
    t0 = start // tm
    assert start % tm == 0 and m % tm == 0
    vec = pl.BlockSpec((None, 1, d), lambda i: (_mod_row(i + t0, tm, n_ctx_rows), 0, 0))
    par = pl.BlockSpec((1, d), lambda i: (0, 0))
    row_in = pl.BlockSpec((tm, d), lambda i: (i + t0, 0))
    col_in = pl.BlockSpec((d, tm), lambda i: (0, i + t0))
    row = pl.BlockSpec((tm, d), lambda i: (i, 0))
    col = pl.BlockSpec((d, tm), lambda i: (0, i))
    in_specs = [row_in, col_in if y_transposed else row_in, vec, par, par, vec, vec]
    args = [x, y, g, ln_g.reshape(1, d), ln_b.reshape(1, d), sh, sc]
    if w is not None:
        in_specs.append(pl.BlockSpec((None, d, d), lambda i: (layer, 0, 0)))
        args.append(w)
    return pl.pallas_call(
        functools.partial(_res_ln_kernel, y_transposed=y_transposed, h_transposed=h_transposed,
                          project=w is not None),
        grid=(m // tm,),
        in_specs=in_specs,
        out_specs=[row, col if h_transposed else row],
        out_shape=[jax.ShapeDtypeStruct((m, d), F32),
                   jax.ShapeDtypeStruct((d, m) if h_transposed else (m, d), BF16)],
        compiler_params=_cparams(1, 48 * 2**20),
        name=name,
    )(*args)


def _mixer_a_kernel(a_ref, lg_ref, lb_ref, ws_ref, bs_ref, y_ref, *, n_chunks):
    gw = A_WIDTH // A_GROUPS
    for c in range(n_chunks):
        rows = slice(c * CHUNK, (c + 1) * CHUNK)
        g = jax.nn.gelu(a_ref[rows, :].astype(F32))
        u = g[:, :A_WIDTH]
        v = g[:, A_WIDTH:]
        mu = jnp.mean(v, axis=-1, keepdims=True)
        vc = v - mu
        var = jnp.mean(vc * vc, axis=-1, keepdims=True)
        v = vc * lax.rsqrt(var + EPS) * lg_ref[...] + lb_ref[...]
        for gi in range(A_GROUPS):
            cols = slice(gi * gw, (gi + 1) * gw)
            sp = _dot(ws_ref[gi].astype(BF16), v[:, cols].astype(BF16)) + bs_ref[gi]
            y_ref[rows, cols] = (u[:, cols] * sp).astype(BF16)


def _mixer_a(z, ln_g, ln_b, ws, bs, tm=256):
    m = z.shape[0]
    gw = A_WIDTH // A_GROUPS
    bs_b = jnp.broadcast_to(bs[:, :, None], (A_GROUPS, CHUNK, gw))
    return pl.pallas_call(
        functools.partial(_mixer_a_kernel, n_chunks=tm // CHUNK),
        grid=(m // tm,),
        in_specs=[
            pl.BlockSpec((tm, 2 * A_WIDTH), lambda i: (i, COL_A // (2 * A_WIDTH))),
            pl.BlockSpec((1, A_WIDTH), lambda i: (0, 0)),
            pl.BlockSpec((1, A_WIDTH), lambda i: (0, 0)),
            pl.BlockSpec((A_GROUPS, CHUNK, CHUNK), lambda i: (0, 0, 0)),
            pl.BlockSpec((A_GROUPS, CHUNK, gw), lambda i: (0, 0, 0)),
        ],
        out_specs=pl.BlockSpec((tm, A_WIDTH), lambda i: (i, 0)),
        out_shape=jax.ShapeDtypeStruct((m, A_WIDTH), BF16),
        compiler_params=_cparams(1, 32 * 2**20),
        name="mixer_a",
    )(z, ln_g.reshape(1, A_WIDTH), ln_b.reshape(1, A_WIDTH), ws, bs_b)


def _attn_ctx_kernel(q_ref, k_ref, v_ref, y_ref, ko_ref, vo_ref):
    for h in range(NA_HEADS):
        cols = slice(h * NA_DH, (h + 1) * NA_DH)
        q, k, v = q_ref[:, cols], k_ref[:, cols], v_ref[:, cols]
        s = _dot_nt(q, k) * (NA_DH ** -0.5)
        m = jnp.max(s, axis=-1, keepdims=True)
        p = jnp.exp(s - m)
        l = jnp.sum(p, axis=-1, keepdims=True)
        o = _dot(p.astype(BF16), v) / l
        y_ref[:, cols] = o.astype(BF16)
        ko_ref[h] = k.astype(F32)
        vo_ref[h] = v.astype(F32)


def _attn_ctx(z, n_ctx):
    width = NA_HEADS * NA_DH
    assert COL_QB % width == 0 and COL_KB % width == 0 and COL_VB % width == 0
    cache = jax.ShapeDtypeStruct((n_ctx, NA_HEADS, SEQ, NA_DH), F32)
    cache_spec = pl.BlockSpec((None, NA_HEADS, SEQ, NA_DH), lambda b: (b, 0, 0, 0))
    return pl.pallas_call(
        _attn_ctx_kernel,
        grid=(n_ctx,),
        in_specs=[
            pl.BlockSpec((SEQ, width), lambda b: (b, COL_QB // width)),
            pl.BlockSpec((SEQ, width), lambda b: (b, COL_KB // width)),
            pl.BlockSpec((SEQ, width), lambda b: (b, COL_VB // width)),
        ],
        out_specs=[pl.BlockSpec((SEQ, width), lambda b: (b, 0)), cache_spec, cache_spec],
        out_shape=[jax.ShapeDtypeStruct((n_ctx * SEQ, width), BF16), cache, cache],
        compiler_params=_cparams(1, 24 * 2**20),
        name="attn_ctx",
    )(z, z, z)


NA_QROWS = 4
NA_KROWS = 12
NA_NQB = (DEC_SEQ // GRID_W) // NA_QROWS


def _na_block(qb):
    n_rows = DEC_SEQ // GRID_W
    k0 = min(max(NA_QROWS * qb - NA_WIN_H // 2, 0), n_rows - NA_KROWS)
    case = 0 if qb == 0 else (2 if qb == NA_NQB - 1 else 1)
    return k0, case


def _rpb_expand_kernel(rp_ref, oh_ref, neg_ref, o_ref):
    r = rp_ref[...]
    hi = r.astype(BF16)
    r1 = r - hi.astype(F32)
    mid = r1.astype(BF16)
    lo = (r1 - mid.astype(F32)).astype(BF16)
    oh = oh_ref[...]
    o_ref[...] = (_dot(hi, oh) + _dot(mid, oh)) + (_dot(lo, oh) + neg_ref[...])


def _na_bias_tables(rpb):
    n_layers, n_heads, n_dr, n_dc = rpb.shape
    cols = np.arange(GRID_W)
    cs = np.clip(cols - NA_WIN_W // 2, 0, GRID_W - NA_WIN_W)
    ok_c = (cols[None, :] >= cs[:, None]) & (cols[None, :] < cs[:, None] + NA_WIN_W)
    dc = np.clip(cols[None, :] - cols[:, None], -(NA_WIN_W - 1), NA_WIN_W - 1) + NA_WIN_W - 1
    onehot = (np.arange(32)[:, None] == dc.reshape(1, -1)).astype(np.float32)
    negmask = np.where(ok_c.reshape(1, -1), 0.0, NEG).astype(np.float32)
    rows = n_layers * n_heads * n_dr
    rows_pad = -(-rows // 128) * 128
    rp = jnp.pad(rpb.reshape(rows, n_dc), ((0, rows_pad - rows), (0, 32 - n_dc)))
    t = pl.pallas_call(
        _rpb_expand_kernel,
        grid=(rows_pad // 128,),
        in_specs=[pl.BlockSpec((128, 32), lambda i: (i, 0)),
                  pl.BlockSpec((32, GRID_W * GRID_W), lambda i: (0, 0)),
                  pl.BlockSpec((1, GRID_W * GRID_W), lambda i: (0, 0))],
        out_specs=pl.BlockSpec((128, GRID_W * GRID_W), lambda i: (i, 0)),
        out_shape=jax.ShapeDtypeStruct((rows_pad, GRID_W * GRID_W), F32),
        compiler_params=_cparams(1, 16 * 2**20),
        name="rpb_expand",
    )(rp, jnp.asarray(onehot, BF16), jnp.asarray(negmask))
    t = t[:rows].reshape(n_layers, n_heads, n_dr, GRID_W, GRID_W)
    n_rows = DEC_SEQ // GRID_W
    neg_blk = jnp.full((n_layers, n_heads, GRID_W, GRID_W), NEG, F32)
    cases = []
    for qb in (0, 1, NA_NQB - 1):
        k0, _ = _na_block(qb)
        q_rows = []
        for qr in range(NA_QROWS):
            r = NA_QROWS * qb + qr
            rs = min(max(r - NA_WIN_H // 2, 0), n_rows - NA_WIN_H)
            blks = []
            for j in range(NA_KROWS):
                kr = k0 + j
                blks.append(t[:, :, kr - r + NA_WIN_H - 1] if rs <= kr < rs + NA_WIN_H else neg_blk)
            q_rows.append(jnp.concatenate(blks, axis=-1))
        cases.append(jnp.concatenate(q_rows, axis=-2))
    return jnp.stack(cases, axis=2)


def _attn_lat_kernel(q_ref, k_ref, v_ref, ck_ref, cv_ref, bias_ref, y_ref):
    scale = NA_DH ** -0.5
    ck = ck_ref[...].astype(BF16)
    cv = cv_ref[...].astype(BF16)
    nq = NA_QROWS * GRID_W
    nk = NA_KROWS * GRID_W
    for qb in range(NA_NQB):
        k0, case = _na_block(qb)
        q = q_ref[qb * nq:(qb + 1) * nq, :]
        kw = k_ref[k0 * GRID_W:k0 * GRID_W + nk, :]
        vw = v_ref[k0 * GRID_W:k0 * GRID_W + nk, :]
        sw = _dot_nt(q, kw) * scale + bias_ref[case]
        sc = _dot_nt(q, ck) * scale
        m = jnp.maximum(jnp.max(sw, axis=-1, keepdims=True), jnp.max(sc, axis=-1, keepdims=True))
        pw = jnp.exp(sw - m)
        pc = jnp.exp(sc - m)
        l = jnp.sum(pw, axis=-1, keepdims=True) + jnp.sum(pc, axis=-1, keepdims=True)
        o = _dot(pw.astype(BF16), vw) + _dot(pc.astype(BF16), cv)
        y_ref[qb * nq:(qb + 1) * nq, :] = (o / l).astype(BF16)


def _attn_lat(z, cache_k, cache_v, bias_tab, layer, n_lat, row_blk0):
    qb, kb, vb = COL_QB // NA_DH, COL_KB // NA_DH, COL_VB // NA_DH
    past = cache_k.shape[3]
    cspec = pl.BlockSpec((None, None, None, past, NA_DH), lambda b, h: (b, layer, h, 0, 0))
    return pl.pallas_call(
        _attn_lat_kernel,
        grid=(n_lat, NA_HEADS),
        in_specs=[
            pl.BlockSpec((DEC_SEQ, NA_DH), lambda b, h: (row_blk0 + b, qb + h)),
            pl.BlockSpec((DEC_SEQ, NA_DH), lambda b, h: (row_blk0 + b, kb + h)),
            pl.BlockSpec((DEC_SEQ, NA_DH), lambda b, h: (row_blk0 + b, vb + h)),
            cspec, cspec,
            pl.BlockSpec((None, None, 3, NA_QROWS * GRID_W, NA_KROWS * GRID_W), lambda b, h: (layer, h, 0, 0, 0)),
        ],
        out_specs=pl.BlockSpec((DEC_SEQ, NA_DH), lambda b, h: (b, h)),
        out_shape=jax.ShapeDtypeStruct((n_lat * DEC_SEQ, NA_HEADS * NA_DH), BF16),
        compiler_params=_cparams(2, 40 * 2**20),
        name="attn_lat",
    )(z, z, z, cache_k, cache_v, bias_tab)


def _rope_tables(t_len):
    half = GLA_DK // 2
    nf = half // 2
    t = np.arange(t_len)
    inv = ROPE_BASE ** (-np.arange(nf, dtype=np.float32) / nf)
    ang_r = (t // GRID_W).astype(np.float32)[:, None] * inv
    ang_c = (t % GRID_W).astype(np.float32)[:, None] * inv
    cos = np.concatenate([np.cos(ang_r), np.cos(ang_r), np.cos(ang_c), np.cos(ang_c)], axis=-1)
    sin = np.concatenate([-np.sin(ang_r), np.sin(ang_r), -np.sin(ang_c), np.sin(ang_c)], axis=-1)
    return jnp.asarray(cos, F32), jnp.asarray(sin, F32)


def _log_sigmoid(x):
    return jnp.minimum(x, 0.0) - jnp.log(1.0 + jnp.exp(-jnp.abs(x)))


def _gla_kernel(*refs, t_len, rope, has_s0, hp):
    refs = list(refs)
    q_ref, k_ref, v_ref, r_ref, lr_ref, wgf_ref, wgb_ref, bgf_ref, bgb_ref, ng_ref = refs[:10]
    pos = 10
    if rope:
        cos_ref, sin_ref = refs[pos:pos + 2]
        pos += 2
    if has_s0:
        s0_ref = refs[pos]
        pos += 1
    y_ref, sfin_ref, qs, ks, gfs, gbs, of, ob = refs[pos:pos + 8]

    gc = GLA_CHUNK
    n_chunks = t_len // gc
    inv_temp = 1.0 / GLA_GATE_TEMP
    lrb = lr_ref[...].astype(BF16)
    if rope:
        lane = lax.broadcasted_iota(jnp.int32, (t_len, GLA_DK), 1)
        first = (lane % (GLA_DK // 2)) < (GLA_DK // 4)

        def swap(x):
            return jnp.where(first, pltpu.roll(x, GLA_DK - GLA_DK // 4, 1), pltpu.roll(x, GLA_DK // 4, 1))

    for hh in range(hp):
        kcols = slice(hh * GLA_DK, (hh + 1) * GLA_DK)
        q = q_ref[:, kcols].astype(F32)
        k = k_ref[:, kcols].astype(F32)
        if rope:
            q = q * cos_ref[...] + swap(q) * sin_ref[...]
            k = k * cos_ref[...] + swap(k) * sin_ref[...]
        qs[:, kcols] = q * (GLA_DK ** -0.5)
        ks[:, kcols] = k
        gfs[:, kcols] = _log_sigmoid(_dot(lrb, wgf_ref[:, kcols].astype(BF16)) + bgf_ref[:, kcols]) * inv_temp
        gbs[:, kcols] = _log_sigmoid(_dot(lrb, wgb_ref[:, kcols].astype(BF16)) + bgb_ref[:, kcols]) * inv_temp

    row = lax.broadcasted_iota(jnp.int32, (gc, gc), 0)
    col = lax.broadcasted_iota(jnp.int32, (gc, gc), 1)
    mid = gc // 2

    def chunk(c, hh, st, g_ref, causal):
        sl = pl.ds(pl.multiple_of(c * gc, gc), gc)
        kcols = slice(hh * GLA_DK, (hh + 1) * GLA_DK)
        vcols = slice(hh * GLA_DV, (hh + 1) * GLA_DV)
        mask = (row >= col) if causal else (row <= col)
        tri = mask.astype(BF16)
        qc, kc, vc, g = qs[sl, kcols], ks[sl, kcols], v_ref[sl, vcols], g_ref[sl, kcols]
        ghi, glo = _split_bf16(g)
        b = _dot(tri, ghi) + _dot(tri, glo)
        bmid = b[mid:mid + 1, :]
        tot = b[gc - 1:gc, :] if causal else b[0:1, :]
        q_in = qc * jnp.exp(b)
        qg = qc * jnp.exp(b - bmid)
        kg = kc * jnp.exp(bmid - b)
        kd = kc * jnp.exp(tot - b)
        sc = jnp.where(mask, _dot_nt(qg.astype(BF16), kg.astype(BF16)), 0.0)
        o = _dot(sc.astype(BF16), vc) + _dot_nt(q_in.astype(BF16), st.astype(BF16))
        st_new = jnp.exp(tot) * st + _dot_tn(vc, kd.astype(BF16))
        return o, st_new

    if has_s0:
        init = tuple(s0_ref[dr, hh].T for hh in range(hp) for dr in range(2))
    else:
        init = tuple(jnp.zeros((GLA_DV, GLA_DK), F32) for _ in range(2 * hp))

    def scan_step(i, carry):
        cb = n_chunks - 1 - i
        out = []
        for hh in range(hp):
            vcols = slice(hh * GLA_DV, (hh + 1) * GLA_DV)
            o_f, st_f = chunk(i, hh, carry[2 * hh], gfs, True)
            o_b, st_b = chunk(cb, hh, carry[2 * hh + 1], gbs, False)
            of[pl.ds(pl.multiple_of(i * gc, gc), gc), vcols] = o_f
            ob[pl.ds(pl.multiple_of(cb * gc, gc), gc), vcols] = o_b
            out += [st_f, st_b]
        return tuple(out)

    final = lax.fori_loop(0, n_chunks, scan_step, init)
    for hh in range(hp):
        sfin_ref[0, hh] = final[2 * hh].T
        sfin_ref[1, hh] = final[2 * hh + 1].T

    def finish(c, carry):
        sl = pl.ds(pl.multiple_of(c * gc, gc), gc)
        for hh in range(hp):
            vcols = slice(hh * GLA_DV, (hh + 1) * GLA_DV)
            o = of[sl, vcols] + ob[sl, vcols]
            o = o * lax.rsqrt(jnp.mean(o * o, axis=-1, keepdims=True) + EPS) * ng_ref[:, vcols]
            r = r_ref[sl, vcols].astype(F32)
            y_ref[sl, vcols] = (o * (r * jax.nn.sigmoid(r))).astype(BF16)
        return carry

    lax.fori_loop(0, n_chunks, finish, 0)


def _gla(z, lr, wgf, wgb, bgf, bgb, norm_g, n_batch, t_len, row_blk0, name, hp, rope_tabs=None, s0=None, layer=0):
    kw, vw = hp * GLA_DK, hp * GLA_DV
    assert GLA_HEADS % hp == 0 and COL_QC % kw == 0 and COL_KC % kw == 0 and COL_VC % vw == 0 and COL_RC % vw == 0
    qc, kc, vc, rc = COL_QC // kw, COL_KC // kw, COL_VC // vw, COL_RC // vw
    kdim = GLA_HEADS * GLA_DK
    in_specs = [
        pl.BlockSpec((t_len, kw), lambda b, h: (row_blk0 + b, qc + h)),
        pl.BlockSpec((t_len, kw), lambda b, h: (row_blk0 + b, kc + h)),
        pl.BlockSpec((t_len, vw), lambda b, h: (row_blk0 + b, vc + h)),
        pl.BlockSpec((t_len, vw), lambda b, h: (row_blk0 + b, rc + h)),
        pl.BlockSpec((t_len, 128), lambda b, h: (row_blk0 + b, 0)),
        pl.BlockSpec((128, kw), lambda b, h: (0, h)),
        pl.BlockSpec((128, kw), lambda b, h: (0, h)),
        pl.BlockSpec((1, kw), lambda b, h: (0, h)),
        pl.BlockSpec((1, kw), lambda b, h: (0, h)),
        pl.BlockSpec((1, vw), lambda b, h: (0, h)),
    ]
    args = [z, z, z, z, lr, wgf, wgb, bgf.reshape(1, kdim), bgb.reshape(1, kdim),
            norm_g.reshape(1, GLA_HEADS * GLA_DV)]
    if rope_tabs is not None:
        in_specs += [pl.BlockSpec((t_len, GLA_DK), lambda b, h: (0, 0))] * 2
        args += list(rope_tabs)
    if s0 is not None:
        in_specs.append(pl.BlockSpec((None, None, 2, hp, GLA_DK, GLA_DV), lambda b, h: (b, layer, 0, h, 0, 0)))
        args.append(s0)
    return pl.pallas_call(
        functools.partial(_gla_kernel, t_len=t_len, rope=rope_tabs is not None, has_s0=s0 is not None, hp=hp),
        grid=(n_batch, GLA_HEADS // hp),
        in_specs=in_specs,
        out_specs=[
            pl.BlockSpec((t_len, vw), lambda b, h: (b, h)),
            pl.BlockSpec((None, 2, hp, GLA_DK, GLA_DV), lambda b, h: (b, 0, h, 0, 0)),
        ],
        out_shape=[
            jax.ShapeDtypeStruct((n_batch * t_len, GLA_HEADS * GLA_DV), BF16),
            jax.ShapeDtypeStruct((n_batch, 2, GLA_HEADS, GLA_DK, GLA_DV), F32),
        ],
        scratch_shapes=[pltpu.VMEM((t_len, kw), F32)] * 4 + [pltpu.VMEM((t_len, vw), F32)] * 2,
        compiler_params=_cparams(2, 48 * 2**20),
        name=name,
    )(*args)


def _merge_kernel(ya_ref, ybc_ref, ybl_ref, ycc_ref, ycl_ref, wa_ref, wb_ref, wc_ref, ga_ref, gb_ref, gc_ref, o_ref,
                  *, n_ctx_tiles):
    is_ctx = pl.program_id(0) < n_ctx_tiles
    yb = jnp.where(is_ctx, ybc_ref[...], ybl_ref[...])
    yc = jnp.where(is_ctx, ycc_ref[...], ycl_ref[...])
    m = jax.nn.sigmoid(ga_ref[...].astype(F32)) * _dot(ya_ref[...], wa_ref[...])
    m += jax.nn.sigmoid(gb_ref[...].astype(F32)) * _dot(yb, wb_ref[...])
    m += jax.nn.sigmoid(gc_ref[...].astype(F32)) * _dot(yc, wc_ref[...])
    o_ref[...] = m.astype(BF16)


def _merge(ya, yb_c, yb_l, yc_c, yc_l, wa, wb, wc, z, layer, tm=512, tn=1024):
    m, kw = ya.shape
    d = wa.shape[-1]
    nc = yb_c.shape[0] // tm
    nl = yb_l.shape[0] // tm
    assert nc * tm == yb_c.shape[0] and nl * tm == yb_l.shape[0] and nc + nl == m // tm
    y_spec = pl.BlockSpec((tm, kw), lambda i, j: (i, 0))
    ctx_spec = pl.BlockSpec((tm, kw), lambda i, j: (jnp.minimum(i, nc - 1), 0))
    lat_spec = pl.BlockSpec((tm, kw), lambda i, j: (jnp.maximum(i - nc, 0), 0))
    w_spec = pl.BlockSpec((None, kw, tn), lambda i, j: (layer, 0, j))

    def gate_spec(col):
        return pl.BlockSpec((tm, tn), lambda i, j: (i, col // tn + j))

    return pl.pallas_call(
        functools.partial(_merge_kernel, n_ctx_tiles=nc),
        grid=(m // tm, d // tn),
        in_specs=[y_spec, ctx_spec, lat_spec, ctx_spec, lat_spec, w_spec, w_spec, w_spec,
                  gate_spec(COL_GA), gate_spec(COL_GB), gate_spec(COL_GC)],
        out_specs=pl.BlockSpec((tm, tn), lambda i, j: (i, j)),
        out_shape=jax.ShapeDtypeStruct((m, d), BF16),
        compiler_params=_cparams(2, 48 * 2**20),
        name="merge",
    )(ya, yb_c, yb_l, yc_c, yc_l, wa, wb, wc, z, z, z)


def _extract_top(x, count, out_ref, with_rank=False):
    rank = jnp.full(x.shape, float(PEER_TOPK), F32)
    for j in range(count):
        m = jnp.max(x, axis=0, keepdims=True)
        out_ref[j:j + 1, :] = m
        hit = x >= m
        if with_rank and j < PEER_TOPK:
            rank = jnp.where(hit, float(j), rank)
        if j + 1 < count:
            x = jnp.where(hit, NEG, x)
    return rank if with_rank else None


def _route_kernel(q_ref, k1_ref, k2_ref, cnt_ref, r2_ref, e2_ref, c1_ref, t1, t2, c):
    nk = PEER_NKEYS
    k = PEER_TOPK

    def scores(keys, qt):
        kh, kl = _split_bf16(keys)
        qh, ql = _split_bf16(qt)
        return _dot(kh, qh) + (_dot(kh, ql) + _dot(kl, qh))

    s1 = scores(k1_ref[...], q_ref[0:nk, :])
    s2 = scores(k2_ref[...], q_ref[nk:2 * nk, :])
    _extract_top(s1, k + 1, t1)
    rank2 = _extract_top(s2, k + 1, t2, with_rank=True)
    n = s1.shape[1]
    rid8 = lax.broadcasted_iota(jnp.int32, (8, n), 0)
    extra = jnp.where(rid8 == 0, t1[k:k + 1, :] + t2[0:1, :],
                      jnp.where(rid8 == 1, t1[0:1, :] + t2[k:k + 1, :], NEG))
    cand = jnp.concatenate(
        [t1[0:1, :] + t2[0:k, :]] + [t1[a:a + 1, :] + t2[0:8, :] for a in range(1, k)] + [extra], axis=0)
    _extract_top(cand, k + 1, c)
    tau = 0.5 * (c[k - 1:k, :] + c[k:k + 1, :])
    zsum = jnp.sum(jnp.exp(c[0:k, :] - c[0:1, :]), axis=0, keepdims=True)
    th = tau - s1
    cnt = jnp.zeros_like(th)
    for b in range(k):
        cnt = jnp.where(th <= t2[b:b + 1, :], float(b + 1), cnt)
    cnt_ref[...] = cnt
    r2_ref[...] = rank2.astype(BF16)
    e2_ref[...] = jnp.exp(s2 - t2[0:1, :]).astype(BF16)
    c1_ref[...] = jnp.exp(s1 - t1[0:1, :]) / zsum


def _peer_route(qt, k1, k2, nt=512):
    m = qt.shape[1]
    nk = PEER_NKEYS
    out_f32 = jax.ShapeDtypeStruct((PEER_HEADS, nk, m), F32)
    out_b16 = jax.ShapeDtypeStruct((PEER_HEADS, nk, m), BF16)
    ospec = pl.BlockSpec((None, nk, nt), lambda i, h: (h, 0, i))
    kspec = pl.BlockSpec((nk, nk), lambda i, h: (0, 0))
    return pl.pallas_call(
        _route_kernel,
        grid=(m // nt, PEER_HEADS),
        in_specs=[pl.BlockSpec((2 * nk, nt), lambda i, h: (h, i)), kspec, kspec],
        out_specs=[ospec] * 4,
        out_shape=[out_f32, out_b16, out_b16, out_f32],
        scratch_shapes=[pltpu.VMEM((24, nt), F32)] * 3,
        compiler_params=_cparams(2, 24 * 2**20),
        name="peer_route",
    )(qt, k1, k2)


PEER_TE = 1024
PEER_RBLK = 4


def _peer_kernel(ht_ref, cnt_ref, r2_in_ref, e2_in_ref, c1_ref, u_ref, vt_ref, o_ref,
                 act_ref, g_ref, r2_ref, e2_ref, *, te, nt):
    j = pl.program_id(1)
    nk = PEER_NKEYS
    groups = te // nk

    @pl.when(j == 0)
    def _():
        o_ref[...] = jnp.zeros_like(o_ref)
        r2_ref[...] = r2_in_ref[...]
        e2_ref[...] = e2_in_ref[...]

    act_ref[...] = _dot(u_ref[...], ht_ref[...]).astype(BF16)

    def row_bf16(ref, h, r, lanes):
        row = jnp.broadcast_to(ref[h, r:r + 1, lanes], (16, 128)).astype(BF16)
        return pltpu.repeat(row, nk // 16, axis=0)

    for lb in range(nt // 128):
        lanes = slice(lb * 128, (lb + 1) * 128)
        for rb in range(groups // PEER_RBLK):
            ws = [jnp.zeros((nk, 128), BF16) for _ in range(PEER_RBLK)]
            for h in range(PEER_HEADS):
                r2t = r2_ref[h * nk:(h + 1) * nk, lanes]
                e2t = e2_ref[h * nk:(h + 1) * nk, lanes]
                for rr in range(PEER_RBLK):
                    r = rb * PEER_RBLK + rr
                    hit = r2t < row_bf16(cnt_ref, h, r, lanes)
                    ws[rr] = ws[rr] + jnp.where(hit, e2t * row_bf16(c1_ref, h, r, lanes), 0.0)
            for rr in range(PEER_RBLK):
                rows = slice((rb * PEER_RBLK + rr) * nk, (rb * PEER_RBLK + rr + 1) * nk)
                g_ref[rows, lanes] = jax.nn.gelu(act_ref[rows, lanes]) * ws[rr]

    o_ref[...] += _dot(vt_ref[...], g_ref[...])


def _peer(ht, cnt, r2, e2, c1, u, vt, layer, nt=512):
    d, m = ht.shape
    te = PEER_TE
    nk = PEER_NKEYS
    groups = te // nk
    assert groups == 8, "first-key rows of a tile must fill one sublane group"
    rspec = pl.BlockSpec((PEER_HEADS * nk, nt), lambda i, j: (0, i))
    gspec = pl.BlockSpec((PEER_HEADS, groups, nt), lambda i, j: (0, j, i))
    est = (2 * (d * nt * 2 + 2 * PEER_HEADS * nk * nt * 2 + 2 * te * d * 2 + d * nt * 4)
           + 2 * te * nt * 2 + te * nt * 4 + d * nt * 4)
    return pl.pallas_call(
        functools.partial(_peer_kernel, te=te, nt=nt),
        grid=(m // nt, N_EXPERTS // te),
        in_specs=[
            pl.BlockSpec((d, nt), lambda i, j: (0, i)),
            gspec, rspec, rspec, gspec,
            pl.BlockSpec((None, te, d), lambda i, j: (layer, j, 0)),
            pl.BlockSpec((None, None, d, te), lambda i, j: (layer, j, 0, 0)),
        ],
        out_specs=pl.BlockSpec((d, nt), lambda i, j: (0, i)),
        out_shape=jax.ShapeDtypeStruct((d, m), F32),
        scratch_shapes=[pltpu.VMEM((te, nt), BF16), pltpu.VMEM((te, nt), BF16),
                        pltpu.VMEM((PEER_HEADS * nk, nt), BF16), pltpu.VMEM((PEER_HEADS * nk, nt), BF16)],
        compiler_params=_cparams(2, est + 2 * PEER_HEADS * nk * nt * 2 + 6 * 2**20),
        name="peer_experts",
    )(ht, cnt, r2.reshape(PEER_HEADS * nk, m), e2.reshape(PEER_HEADS * nk, m), c1, u, vt)


def kernel(x_prompt, x_sample, cache_k, cache_v, state_gla, c, c_ctx, w_ada, b_ada, w_in, a_ln_g, a_ln_b, a_ws, a_bs, na_rpb, gla_wg_f, gla_bg_f, gla_wg_b, gla_bg_b, gla_norm_g, w_br_a, w_br_b, w_br_c, w_out, ln1_g, ln1_b, ln2_g, ln2_b, peer_wq, peer_k1, peer_k2, peer_u, peer_v):
    n_ctx, n_lat = x_prompt.shape[0], x_sample.shape[0]
    n_layers = w_in.shape[0]
    d = D_MODEL
    mc, ml = n_ctx * SEQ, n_lat * DEC_SEQ
    assert mc % DEC_SEQ == 0, "latent row blocks must start on a DEC_SEQ boundary of the shared token axis"
    assert n_lat + 1 <= 16

    cc = jnp.zeros((16, d), F32).at[0].set(c_ctx).at[1:1 + n_lat].set(c)
    mod = _ada(cc, w_ada, b_ada).reshape(n_layers, 16, N_MOD, 1, d)

    def mod_vec(layer, which):
        return mod[layer, :, which]

    w_main = jnp.concatenate([w_in[:, :, :LR_START], w_in[:, :, LR_END:]], axis=-1).astype(BF16)
    w_lr = jnp.pad(w_in[:, :, LR_START:LR_END], ((0, 0), (0, 0), (0, 128 - (LR_END - LR_START)))).astype(BF16)
    wgf = jnp.pad(gla_wg_f, ((0, 0), (0, 128 - GLA_RANK), (0, 0)))
    wgb = jnp.pad(gla_wg_b, ((0, 0), (GLA_RANK, 128 - 2 * GLA_RANK), (0, 0)))
    wbr_a, wbr_b, wbr_c = w_br_a.astype(BF16), w_br_b.astype(BF16), w_br_c.astype(BF16)
    w_o = w_out.astype(BF16)
    wq_t = jnp.swapaxes(peer_wq, 1, 2).astype(BF16)
    u_b = peer_u.astype(BF16)
    v_t = jnp.swapaxes(peer_v.reshape(n_layers, N_EXPERTS // PEER_TE, PEER_TE, d), 2, 3).astype(BF16)
    rope_tabs = _rope_tables(DEC_SEQ)
    bias_tab = _na_bias_tables(na_rpb)

    x = jnp.concatenate([x_prompt.reshape(mc, d), x_sample.reshape(ml, d)], axis=0)
    h = _modulate(x, mod_vec(0, 0), mod_vec(0, 1), mc)

    new_k, new_v, new_s = [], [], []
    for l in range(n_layers):
        z = _matmul(h, w_main, BF16, 512, 2048, "in_proj", b_resident=True, layer=l)
        lr = _matmul(h, w_lr[l], F32, 2048, 128, "lr_proj")
        ya = _mixer_a(z, a_ln_g[l], a_ln_b[l], a_ws[l], a_bs[l])
        yb_c, k_l, v_l = _attn_ctx(z, n_ctx)
        yb_l = _attn_lat(z, cache_k, cache_v, bias_tab, l, n_lat, mc // DEC_SEQ)
        gla_args = (wgf[l], wgb[l], gla_bg_f[l], gla_bg_b[l], gla_norm_g[l])
        yc_c, s_l = _gla(z, lr, *gla_args, n_ctx, SEQ, 0, "gla_ctx", 4)
        yc_l, _ = _gla(z, lr, *gla_args, n_lat, DEC_SEQ, mc // DEC_SEQ, "gla_lat", 2,
                       rope_tabs=rope_tabs, s0=state_gla, layer=l)
        new_k.append(k_l)
        new_v.append(v_l)
        new_s.append(s_l)
        mrg = _merge(ya, yb_c, yb_l, yc_c, yc_l, wbr_a, wbr_b, wbr_c, z, l)
        x, h2t = _res_ln(x, mrg, mod_vec(l, 2), ln1_g[l], ln1_b[l], mod_vec(l, 3), mod_vec(l, 4), mc, "out_ln1",
                         h_transposed=True, w=w_o, layer=l, tm=512)
        qt = _matmul(wq_t, h2t, F32, 512, 1024, "peer_q", layer=l)
        cnt, r2, e2, c1 = _peer_route(qt, peer_k1[l], peer_k2[l])
        pt = _peer(h2t, cnt, r2, e2, c1, u_b, v_t, l)
        ln2_args = (x, pt, mod_vec(l, 5), ln2_g[l], ln2_b[l])
        if l + 1 < n_layers:
            x, h = _res_ln(*ln2_args, mod_vec(l + 1, 0), mod_vec(l + 1, 1), mc, "res_ln2", y_transposed=True)
        else:
            x_ctx, _ = _res_ln(*ln2_args, mod_vec(l, 0), mod_vec(l, 1), mc, "res_ln2_ctx", y_transposed=True,
                               rows=(0, mc))
            x_lat, _ = _res_ln(*ln2_args, mod_vec(l, 0), mod_vec(l, 1), mc, "res_ln2_lat", y_transposed=True,
                               rows=(mc, ml))

    y_prompt = x_ctx.reshape(n_ctx, SEQ, d)
    y_sample = x_lat.reshape(n_lat, DEC_SEQ, d)
    return (y_prompt, y_sample, jnp.stack(new_k, axis=1), jnp.stack(new_v, axis=1), jnp.stack(new_s, axis=1))
```

```python
import functools

import numpy as np
import jax
import jax.numpy as jnp
from jax import lax
from jax.experimental import pallas as pl
from jax.experimental.pallas import tpu as pltpu

F32 = jnp.float32
BF16 = jnp.bfloat16

D_MODEL = 2048
SEQ = 256
DEC_SEQ = 2048
GRID_W = 64
CHUNK = 128
A_GROUPS = 8
A_WIDTH = 1024
NA_HEADS = 8
NA_DH = 128
NA_WIN_H = 8
NA_WIN_W = 16
GLA_HEADS = 4
GLA_DK = 128
GLA_DV = 256
GLA_RANK = 16
GLA_GATE_TEMP = 16.0
GLA_CHUNK = 64
ROPE_BASE = 10000.0
PEER_HEADS = 8
PEER_NKEYS = 128
PEER_TOPK = 16
N_EXPERTS = PEER_NKEYS * PEER_NKEYS
N_MOD = 6
ALPHA = 8.0 ** 0.25
EPS = 1e-5
NEG = -1e30

Z_WIDTH = 14336
COL_A = 0
COL_QB, COL_KB, COL_VB = 2048, 3072, 4096
COL_QC, COL_KC, COL_VC, COL_RC = 5120, 5632, 6144, 7168
COL_GA, COL_GB, COL_GC = 8192, 10240, 12288
LR_START, LR_END = 8192, 8224

VMEM_CAP = 56 * 1024 * 1024


def _cparams(n_axes, vmem_bytes, flags=None):
    return pltpu.CompilerParams(
        dimension_semantics=("arbitrary",) * n_axes,
        vmem_limit_bytes=min(int(vmem_bytes), VMEM_CAP),
        flags=flags,
    )


def _mod_row(i, tm, n_ctx_rows):
    start = i * tm
    return jnp.where(start < n_ctx_rows, 0, 1 + (start - n_ctx_rows) // DEC_SEQ)


def _dot(a, b):
    return jnp.dot(a, b, preferred_element_type=F32)


def _dot_nt(a, b):
    return lax.dot_general(a, b, (((1,), (1,)), ((), ())), preferred_element_type=F32)


def _dot_tn(a, b):
    return lax.dot_general(a, b, (((0,), (0,)), ((), ())), preferred_element_type=F32)


def _split_bf16(x):
    hi = x.astype(BF16)
    lo = (x - hi.astype(F32)).astype(BF16)
    return hi, lo


def _ada_kernel(c_ref, w_ref, b_ref, o_ref):
    c = c_ref[...]
    a = (c * jax.nn.sigmoid(c)).astype(BF16)
    o_ref[...] = _dot(a, w_ref[...].astype(BF16)) + b_ref[...]


def _ada(cc, w_ada, b_ada):
    n_layers, d, n = w_ada.shape
    rows = cc.shape[0]
    tn = 1024
    return pl.pallas_call(
        _ada_kernel,
        grid=(n_layers, n // tn),
        in_specs=[
            pl.BlockSpec((rows, d), lambda l, j: (0, 0)),
            pl.BlockSpec((None, d, tn), lambda l, j: (l, 0, j)),
            pl.BlockSpec((None, 1, tn), lambda l, j: (l, 0, j)),
        ],
        out_specs=pl.BlockSpec((None, rows, tn), lambda l, j: (l, 0, j)),
        out_shape=jax.ShapeDtypeStruct((n_layers, rows, n), F32),
        compiler_params=_cparams(2, 2 * d * tn * 4 + 12 * 2**20),
        name="ada_mod",
    )(cc, w_ada, b_ada.reshape(n_layers, 1, n))


def _mm_kernel(a_ref, b_ref, o_ref):
    o_ref[...] = _dot(a_ref[...].astype(BF16), b_ref[...].astype(BF16)).astype(o_ref.dtype)


def _matmul(a, b, out_dtype, tm, tn, name, b_resident=False, layer=None):
    m, k = a.shape[-2:]
    n = b.shape[-1]
    assert m % tm == 0 and n % tn == 0, (m, n, tm, tn)
    est = 2 * (tm * k * a.dtype.itemsize + k * tn * b.dtype.itemsize + tm * tn * 4) + tm * tn * 8
    if b_resident:
        grid = (n // tn, m // tm)
        a_idx, b_idx, o_map = (lambda j, i: (i, 0)), (lambda j, i: (0, j)), (lambda j, i: (i, j))
    else:
        grid = (m // tm, n // tn)
        a_idx, b_idx, o_map = (lambda i, j: (i, 0)), (lambda i, j: (0, j)), (lambda i, j: (i, j))

    def spec(x, block, idx):
        if x.ndim == 2:
            return pl.BlockSpec(block, idx)
        return pl.BlockSpec((None,) + block, lambda *g: (layer,) + idx(*g))

    return pl.pallas_call(
        _mm_kernel,
        grid=grid,
        in_specs=[spec(a, (tm, k), a_idx), spec(b, (k, tn), b_idx)],
        out_specs=pl.BlockSpec((tm, tn), o_map),
        out_shape=jax.ShapeDtypeStruct((m, n), out_dtype),
        compiler_params=_cparams(2, est + 8 * 2**20),
        name=name,
    )(a, b)


def _modulate_kernel(x_ref, sh_ref, sc_ref, h_ref):
    h_ref[...] = (x_ref[...] * (1.0 + sc_ref[...]) + sh_ref[...]).astype(BF16)


def _modulate(x, sh, sc, n_ctx_rows, tm=256):
    m, d = x.shape
    vec = pl.BlockSpec((None, 1, d), lambda i: (_mod_row(i, tm, n_ctx_rows), 0, 0))
    return pl.pallas_call(
        _modulate_kernel,
        grid=(m // tm,),
        in_specs=[pl.BlockSpec((tm, d), lambda i: (i, 0)), vec, vec],
        out_specs=pl.BlockSpec((tm, d), lambda i: (i, 0)),
        out_shape=jax.ShapeDtypeStruct((m, d), BF16),
        compiler_params=_cparams(1, 32 * 2**20),
        name="modulate",
    )(x, sh, sc)


def _res_ln_kernel(x_ref, y_ref, g_ref, lg_ref, lb_ref, sh_ref, sc_ref, *rest, y_transposed, h_transposed, project):
    if project:
        w_ref, xo_ref, h_ref = rest
        y = _dot(y_ref[...], w_ref[...])
    else:
        xo_ref, h_ref = rest
        y = y_ref[...].astype(F32)
    if y_transposed:
        y = y.T
    t = ALPHA * x_ref[...] + g_ref[...] * y
    mu = jnp.mean(t, axis=-1, keepdims=True)
    tc = t - mu
    var = jnp.mean(tc * tc, axis=-1, keepdims=True)
    xn = tc * lax.rsqrt(var + EPS) * lg_ref[...] + lb_ref[...]
    xo_ref[...] = xn
    h = xn * (1.0 + sc_ref[...]) + sh_ref[...]
    h_ref[...] = (h.T if h_transposed else h).astype(BF16)


def _res_ln(x, y, g, ln_g, ln_b, sh, sc, n_ctx_rows, name, tm=256, y_transposed=False, h_transposed=False,
            rows=None, w=None, layer=None):
    d = x.shape[1]
    start, m = (0, x.shape[0]) if rows is None else rows
    t0 = start // tm
    assert start % tm == 0 and m % tm == 0
    vec = pl.BlockSpec((None, 1, d), lambda i: (_mod_row(i + t0, tm, n_ctx_rows), 0, 0))
    par = pl.BlockSpec((1, d), lambda i: (0, 0))
    row_in = pl.BlockSpec((tm, d), lambda i: (i + t0, 0))
    col_in = pl.BlockSpec((d, tm), lambda i: (0, i + t0))
    row = pl.BlockSpec((tm, d), lambda i: (i, 0))
    col = pl.BlockSpec((d, tm), lambda i: (0, i))
    in_specs = [row_in, col_in if y_transposed else row_in, vec, par, par, vec, vec]
    args = [x, y, g, ln_g.reshape(1, d), ln_b.reshape(1, d), sh, sc]
    if w is not None:
        in_specs.append(pl.BlockSpec((None, d, d), lambda i: (layer, 0, 0)))
        args.append(w)
    return pl.pallas_call(
        functools.partial(_res_ln_kernel, y_transposed=y_transposed, h_transposed=h_transposed,
                          project=w is not None),
        grid=(m // tm,),
        in_specs=in_specs,
        out_specs=[row, col if h_transposed else row],
        out_shape=[jax.ShapeDtypeStruct((m, d), F32),
                   jax.ShapeDtypeStruct((d, m) if h_transposed else (m, d), BF16)],
        compiler_params=_cparams(1, 48 * 2**20),
        name=name,
    )(*args)


def _mixer_a_kernel(a_ref, lg_ref, lb_ref, ws_ref, bs_ref, y_ref, *, n_chunks):
    gw = A_WIDTH // A_GROUPS
    for c in range(n_chunks):
        rows = slice(c * CHUNK, (c + 1) * CHUNK)
        g = jax.nn.gelu(a_ref[rows, :].astype(F32))
        u = g[:, :A_WIDTH]
        v = g[:, A_WIDTH:]
        mu = jnp.mean(v, axis=-1, keepdims=True)
        vc = v - mu
        var = jnp.mean(vc * vc, axis=-1, keepdims=True)
        v = vc * lax.rsqrt(var + EPS) * lg_ref[...] + lb_ref[...]
        for gi in range(A_GROUPS):
            cols = slice(gi * gw, (gi + 1) * gw)
            sp = _dot(ws_ref[gi].astype(BF16), v[:, cols].astype(BF16)) + bs_ref[gi]
            y_ref[rows, cols] = (u[:, cols] * sp).astype(BF16)


def _mixer_a(z, ln_g, ln_b, ws, bs, tm=256):
    m = z.shape[0]
    gw = A_WIDTH // A_GROUPS
    bs_b = jnp.broadcast_to(bs[:, :, None], (A_GROUPS, CHUNK, gw))
    return pl.pallas_call(
        functools.partial(_mixer_a_kernel, n_chunks=tm // CHUNK),
        grid=(m // tm,),
        in_specs=[
            pl.BlockSpec((tm, 2 * A_WIDTH), lambda i: (i, COL_A // (2 * A_WIDTH))),
            pl.BlockSpec((1, A_WIDTH), lambda i: (0, 0)),
            pl.BlockSpec((1, A_WIDTH), lambda i: (0, 0)),
            pl.BlockSpec((A_GROUPS, CHUNK, CHUNK), lambda i: (0, 0, 0)),
            pl.BlockSpec((A_GROUPS, CHUNK, gw), lambda i: (0, 0, 0)),
        ],
        out_specs=pl.BlockSpec((tm, A_WIDTH), lambda i: (i, 0)),
        out_shape=jax.ShapeDtypeStruct((m, A_WIDTH), BF16),
        compiler_params=_cparams(1, 32 * 2**20),
        name="mixer_a",
    )(z, ln_g.reshape(1, A_WIDTH), ln_b.reshape(1, A_WIDTH), ws, bs_b)


def _attn_ctx_kernel(q_ref, k_ref, v_ref, y_ref, ko_ref, vo_ref):
    for h in range(NA_HEADS):
        cols = slice(h * NA_DH, (h + 1) * NA_DH)
        q, k, v = q_ref[:, cols], k_ref[:, cols], v_ref[:, cols]
        s = _dot_nt(q, k) * (NA_DH ** -0.5)
        m = jnp.max(s, axis=-1, keepdims=True)
        p = jnp.exp(s - m)
        l = jnp.sum(p, axis=-1, keepdims=True)
        o = _dot(p.astype(BF16), v) / l
        y_ref[:, cols] = o.astype(BF16)
        ko_ref[h] = k.astype(F32)
        vo_ref[h] = v.astype(F32)


def _attn_ctx(z, n_ctx):
    width = NA_HEADS * NA_DH
    assert COL_QB % width == 0 and COL_KB % width == 0 and COL_VB % width == 0
    cache = jax.ShapeDtypeStruct((n_ctx, NA_HEADS, SEQ, NA_DH), F32)
    cache_spec = pl.BlockSpec((None, NA_HEADS, SEQ, NA_DH), lambda b: (b, 0, 0, 0))
    return pl.pallas_call(
        _attn_ctx_kernel,
        grid=(n_ctx,),
        in_specs=[
            pl.BlockSpec((SEQ, width), lambda b: (b, COL_QB // width)),
            pl.BlockSpec((SEQ, width), lambda b: (b, COL_KB // width)),
            pl.BlockSpec((SEQ, width), lambda b: (b, COL_VB // width)),
        ],
        out_specs=[pl.BlockSpec((SEQ, width), lambda b: (b, 0)), cache_spec, cache_spec],
        out_shape=[jax.ShapeDtypeStruct((n_ctx * SEQ, width), BF16), cache, cache],
        compiler_params=_cparams(1, 24 * 2**20),
        name="attn_ctx",
    )(z, z, z)


NA_QROWS = 4
NA_KROWS = 12
NA_NQB = (DEC_SEQ // GRID_W) // NA_QROWS


def _na_block(qb):
    n_rows = DEC_SEQ // GRID_W
    k0 = min(max(NA_QROWS * qb - NA_WIN_H // 2, 0), n_rows - NA_KROWS)
    case = 0 if qb == 0 else (2 if qb == NA_NQB - 1 else 1)
    return k0, case


def _rpb_expand_kernel(rp_ref, oh_ref, neg_ref, o_ref):
    r = rp_ref[...]
    hi = r.astype(BF16)
    r1 = r - hi.astype(F32)
    mid = r1.astype(BF16)
    lo = (r1 - mid.astype(F32)).astype(BF16)
    oh = oh_ref[...]
    o_ref[...] = (_dot(hi, oh) + _dot(mid, oh)) + (_dot(lo, oh) + neg_ref[...])


def _na_bias_tables(rpb):
    n_layers, n_heads, n_dr, n_dc = rpb.shape
    cols = np.arange(GRID_W)
    cs = np.clip(cols - NA_WIN_W // 2, 0, GRID_W - NA_WIN_W)
    ok_c = (cols[None, :] >= cs[:, None]) & (cols[None, :] < cs[:, None] + NA_WIN_W)
    dc = np.clip(cols[None, :] - cols[:, None], -(NA_WIN_W - 1), NA_WIN_W - 1) + NA_WIN_W - 1
    onehot = (np.arange(32)[:, None] == dc.reshape(1, -1)).astype(np.float32)
    negmask = np.where(ok_c.reshape(1, -1), 0.0, NEG).astype(np.float32)
    rows = n_layers * n_heads * n_dr
    rows_pad = -(-rows // 128) * 128
    rp = jnp.pad(rpb.reshape(rows, n_dc), ((0, rows_pad - rows), (0, 32 - n_dc)))
    t = pl.pallas_call(
        _rpb_expand_kernel,
        grid=(rows_pad // 128,),
        in_specs=[pl.BlockSpec((128, 32), lambda i: (i, 0)),
                  pl.BlockSpec((32, GRID_W * GRID_W), lambda i: (0, 0)),
                  pl.BlockSpec((1, GRID_W * GRID_W), lambda i: (0, 0))],
        out_specs=pl.BlockSpec((128, GRID_W * GRID_W), lambda i: (i, 0)),
        out_shape=jax.ShapeDtypeStruct((rows_pad, GRID_W * GRID_W), F32),
        compiler_params=_cparams(1, 16 * 2**20),
        name="rpb_expand",
    )(rp, jnp.asarray(onehot, BF16), jnp.asarray(negmask))
    t = t[:rows].reshape(n_layers, n_heads, n_dr, GRID_W, GRID_W)
    n_rows = DEC_SEQ // GRID_W
    neg_blk = jnp.full((n_layers, n_heads, GRID_W, GRID_W), NEG, F32)
    cases = []
    for qb in (0, 1, NA_NQB - 1):
        k0, _ = _na_block(qb)
        q_rows = []
        for qr in range(NA_QROWS):
            r = NA_QROWS * qb + qr
            rs = min(max(r - NA_WIN_H // 2, 0), n_rows - NA_WIN_H)
            blks = []
            for j in range(NA_KROWS):
                kr = k0 + j
                blks.append(t[:, :, kr - r + NA_WIN_H - 1] if rs <= kr < rs + NA_WIN_H else neg_blk)
            q_rows.append(jnp.concatenate(blks, axis=-1))
        cases.append(jnp.concatenate(q_rows, axis=-2))
    return jnp.stack(cases, axis=2)


def _attn_lat_kernel(q_ref, k_ref, v_ref, ck_ref, cv_ref, bias_ref, y_ref):
    scale = NA_DH ** -0.5
    ck = ck_ref[...].astype(BF16)
    cv = cv_ref[...].astype(BF16)
    nq = NA_QROWS * GRID_W
    nk = NA_KROWS * GRID_W
    for qb in range(NA_NQB):
        k0, case = _na_block(qb)
        q = q_ref[qb * nq:(qb + 1) * nq, :]
        kw = k_ref[k0 * GRID_W:k0 * GRID_W + nk, :]
        vw = v_ref[k0 * GRID_W:k0 * GRID_W + nk, :]
        sw = _dot_nt(q, kw) * scale + bias_ref[case]
        sc = _dot_nt(q, ck) * scale
        m = jnp.maximum(jnp.max(sw, axis=-1, keepdims=True), jnp.max(sc, axis=-1, keepdims=True))
        pw = jnp.exp(sw - m)
        pc = jnp.exp(sc - m)
        l = jnp.sum(pw, axis=-1, keepdims=True) + jnp.sum(pc, axis=-1, keepdims=True)
        o = _dot(pw.astype(BF16), vw) + _dot(pc.astype(BF16), cv)
        y_ref[qb * nq:(qb + 1) * nq, :] = (o / l).astype(BF16)


def _attn_lat(z, cache_k, cache_v, bias_tab, layer, n_lat, row_blk0):
    qb, kb, vb = COL_QB // NA_DH, COL_KB // NA_DH, COL_VB // NA_DH
    past = cache_k.shape[3]
    cspec = pl.BlockSpec((None, None, None, past, NA_DH), lambda b, h: (b, layer, h, 0, 0))
    return pl.pallas_call(
        _attn_lat_kernel,
        grid=(n_lat, NA_HEADS),
        in_specs=[
            pl.BlockSpec((DEC_SEQ, NA_DH), lambda b, h: (row_blk0 + b, qb + h)),
            pl.BlockSpec((DEC_SEQ, NA_DH), lambda b, h: (row_blk0 + b, kb + h)),
            pl.BlockSpec((DEC_SEQ, NA_DH), lambda b, h: (row_blk0 + b, vb + h)),
            cspec, cspec,
            pl.BlockSpec((None, None, 3, NA_QROWS * GRID_W, NA_KROWS * GRID_W), lambda b, h: (layer, h, 0, 0, 0)),
        ],
        out_specs=pl.BlockSpec((DEC_SEQ, NA_DH), lambda b, h: (b, h)),
        out_shape=jax.ShapeDtypeStruct((n_lat * DEC_SEQ, NA_HEADS * NA_DH), BF16),
        compiler_params=_cparams(2, 40 * 2**20),
        name="attn_lat",
    )(z, z, z, cache_k, cache_v, bias_tab)


def _rope_tables(t_len):
    half = GLA_DK // 2
    nf = half // 2
    t = np.arange(t_len)
    inv = ROPE_BASE ** (-np.arange(nf, dtype=np.float32) / nf)
    ang_r = (t // GRID_W).astype(np.float32)[:, None] * inv
    ang_c = (t % GRID_W).astype(np.float32)[:, None] * inv
    cos = np.concatenate([np.cos(ang_r), np.cos(ang_r), np.cos(ang_c), np.cos(ang_c)], axis=-1)
    sin = np.concatenate([-np.sin(ang_r), np.sin(ang_r), -np.sin(ang_c), np.sin(ang_c)], axis=-1)
    return jnp.asarray(cos, F32), jnp.asarray(sin, F32)


def _log_sigmoid(x):
    return jnp.minimum(x, 0.0) - jnp.log(1.0 + jnp.exp(-jnp.abs(x)))


def _gla_kernel(*refs, t_len, rope, has_s0, hp):
    refs = list(refs)
    q_ref, k_ref, v_ref, r_ref, lr_ref, wgf_ref, wgb_ref, bgf_ref, bgb_ref, ng_ref = refs[:10]
    pos = 10
    if rope:
        cos_ref, sin_ref = refs[pos:pos + 2]
        pos += 2
    if has_s0:
        s0_ref = refs[pos]
        pos += 1
    y_ref, sfin_ref, qs, ks, gfs, gbs, of, ob = refs[pos:pos + 8]

    gc = GLA_CHUNK
    n_chunks = t_len // gc
    inv_temp = 1.0 / GLA_GATE_TEMP
    lrb = lr_ref[...].astype(BF16)
    if rope:
        lane = lax.broadcasted_iota(jnp.int32, (t_len, GLA_DK), 1)
        first = (lane % (GLA_DK // 2)) < (GLA_DK // 4)

        def swap(x):
            return jnp.where(first, pltpu.roll(x, GLA_DK - GLA_DK // 4, 1), pltpu.roll(x, GLA_DK // 4, 1))

    for hh in range(hp):
        kcols = slice(hh * GLA_DK, (hh + 1) * GLA_DK)
        q = q_ref[:, kcols].astype(F32)
        k = k_ref[:, kcols].astype(F32)
        if rope:
            q = q * cos_ref[...] + swap(q) * sin_ref[...]
            k = k * cos_ref[...] + swap(k) * sin_ref[...]
        qs[:, kcols] = q * (GLA_DK ** -0.5)
        ks[:, kcols] = k
        gfs[:, kcols] = _log_sigmoid(_dot(lrb, wgf_ref[:, kcols].astype(BF16)) + bgf_ref[:, kcols]) * inv_temp
        gbs[:, kcols] = _log_sigmoid(_dot(lrb, wgb_ref[:, kcols].astype(BF16)) + bgb_ref[:, kcols]) * inv_temp

    row = lax.broadcasted_iota(jnp.int32, (gc, gc), 0)
    col = lax.broadcasted_iota(jnp.int32, (gc, gc), 1)
    mid = gc // 2

    def chunk(c, hh, st, g_ref, causal):
        sl = pl.ds(pl.multiple_of(c * gc, gc), gc)
        kcols = slice(hh * GLA_DK, (hh + 1) * GLA_DK)
        vcols = slice(hh * GLA_DV, (hh + 1) * GLA_DV)
        mask = (row >= col) if causal else (row <= col)
        tri = mask.astype(BF16)
        qc, kc, vc, g = qs[sl, kcols], ks[sl, kcols], v_ref[sl, vcols], g_ref[sl, kcols]
        ghi, glo = _split_bf16(g)
        b = _dot(tri, ghi) + _dot(tri, glo)
        bmid = b[mid:mid + 1, :]
        tot = b[gc - 1:gc, :] if causal else b[0:1, :]
        q_in = qc * jnp.exp(b)
        qg = qc * jnp.exp(b - bmid)
        kg = kc * jnp.exp(bmid - b)
        kd = kc * jnp.exp(tot - b)
        sc = jnp.where(mask, _dot_nt(qg.astype(BF16), kg.astype(BF16)), 0.0)
        o = _dot(sc.astype(BF16), vc) + _dot_nt(q_in.astype(BF16), st.astype(BF16))
        st_new = jnp.exp(tot) * st + _dot_tn(vc, kd.astype(BF16))
        return o, st_new

    if has_s0:
        init = tuple(s0_ref[dr, hh].T for hh in range(hp) for dr in range(2))
    else:
        init = tuple(jnp.zeros((GLA_DV, GLA_DK), F32) for _ in range(2 * hp))

    def scan_step(i, carry):
        cb = n_chunks - 1 - i
        out = []
        for hh in range(hp):
            vcols = slice(hh * GLA_DV, (hh + 1) * GLA_DV)
            o_f, st_f = chunk(i, hh, carry[2 * hh], gfs, True)
            o_b, st_b = chunk(cb, hh, carry[2 * hh + 1], gbs, False)
            of[pl.ds(pl.multiple_of(i * gc, gc), gc), vcols] = o_f
            ob[pl.ds(pl.multiple_of(cb * gc, gc), gc), vcols] = o_b
            out += [st_f, st_b]
        return tuple(out)

    final = lax.fori_loop(0, n_chunks, scan_step, init)
    for hh in range(hp):
        sfin_ref[0, hh] = final[2 * hh].T
        sfin_ref[1, hh] = final[2 * hh + 1].T

    def finish(c, carry):
        sl = pl.ds(pl.multiple_of(c * gc, gc), gc)
        for hh in range(hp):
            vcols = slice(hh * GLA_DV, (hh + 1) * GLA_DV)
            o = of[sl, vcols] + ob[sl, vcols]
            o = o * lax.rsqrt(jnp.mean(o * o, axis=-1, keepdims=True) + EPS) * ng_ref[:, vcols]
            r = r_ref[sl, vcols].astype(F32)
            y_ref[sl, vcols] = (o * (r * jax.nn.sigmoid(r))).astype(BF16)
        return carry

    lax.fori_loop(0, n_chunks, finish, 0)


def _gla(z, lr, wgf, wgb, bgf, bgb, norm_g, n_batch, t_len, row_blk0, name, hp, rope_tabs=None, s0=None, layer=0):
    kw, vw = hp * GLA_DK, hp * GLA_DV
    assert GLA_HEADS % hp == 0 and COL_QC % kw == 0 and COL_KC % kw == 0 and COL_VC % vw == 0 and COL_RC % vw == 0
    qc, kc, vc, rc = COL_QC // kw, COL_KC // kw, COL_VC // vw, COL_RC // vw
    kdim = GLA_HEADS * GLA_DK
    in_specs = [
        pl.BlockSpec((t_len, kw), lambda b, h: (row_blk0 + b, qc + h)),
        pl.BlockSpec((t_len, kw), lambda b, h: (row_blk0 + b, kc + h)),
        pl.BlockSpec((t_len, vw), lambda b, h: (row_blk0 + b, vc + h)),
        pl.BlockSpec((t_len, vw), lambda b, h: (row_blk0 + b, rc + h)),
        pl.BlockSpec((t_len, 128), lambda b, h: (row_blk0 + b, 0)),
        pl.BlockSpec((128, kw), lambda b, h: (0, h)),
        pl.BlockSpec((128, kw), lambda b, h: (0, h)),
        pl.BlockSpec((1, kw), lambda b, h: (0, h)),
        pl.BlockSpec((1, kw), lambda b, h: (0, h)),
        pl.BlockSpec((1, vw), lambda b, h: (0, h)),
    ]
    args = [z, z, z, z, lr, wgf, wgb, bgf.reshape(1, kdim), bgb.reshape(1, kdim),
            norm_g.reshape(1, GLA_HEADS * GLA_DV)]
    if rope_tabs is not None:
        in_specs += [pl.BlockSpec((t_len, GLA_DK), lambda b, h: (0, 0))] * 2
        args += list(rope_tabs)
    if s0 is not None:
        in_specs.append(pl.BlockSpec((None, None, 2, hp, GLA_DK, GLA_DV), lambda b, h: (b, layer, 0, h, 0, 0)))
        args.append(s0)
    return pl.pallas_call(
        functools.partial(_gla_kernel, t_len=t_len, rope=rope_tabs is not None, has_s0=s0 is not None, hp=hp),
        grid=(n_batch, GLA_HEADS // hp),
        in_specs=in_specs,
        out_specs=[
            pl.BlockSpec((t_len, vw), lambda b, h: (b, h)),
            pl.BlockSpec((None, 2, hp, GLA_DK, GLA_DV), lambda b, h: (b, 0, h, 0, 0)),
        ],
        out_shape=[
            jax.ShapeDtypeStruct((n_batch * t_len, GLA_HEADS * GLA_DV), BF16),
            jax.ShapeDtypeStruct((n_batch, 2, GLA_HEADS, GLA_DK, GLA_DV), F32),
        ],
        scratch_shapes=[pltpu.VMEM((t_len, kw), F32)] * 4 + [pltpu.VMEM((t_len, vw), F32)] * 2,
        compiler_params=_cparams(2, 48 * 2**20),
        name=name,
    )(*args)


def _merge_kernel(ya_ref, ybc_ref, ybl_ref, ycc_ref, ycl_ref, wa_ref, wb_ref, wc_ref, ga_ref, gb_ref, gc_ref, o_ref,
                  *, n_ctx_tiles):
    is_ctx = pl.program_id(0) < n_ctx_tiles
    yb = jnp.where(is_ctx, ybc_ref[...], ybl_ref[...])
    yc = jnp.where(is_ctx, ycc_ref[...], ycl_ref[...])
    m = jax.nn.sigmoid(ga_ref[...].astype(F32)) * _dot(ya_ref[...], wa_ref[...])
    m += jax.nn.sigmoid(gb_ref[...].astype(F32)) * _dot(yb, wb_ref[...])
    m += jax.nn.sigmoid(gc_ref[...].astype(F32)) * _dot(yc, wc_ref[...])
    o_ref[...] = m.astype(BF16)


def _merge(ya, yb_c, yb_l, yc_c, yc_l, wa, wb, wc, z, layer, tm=512, tn=1024):
    m, kw = ya.shape
    d = wa.shape[-1]
    nc = yb_c.shape[0] // tm
    nl = yb_l.shape[0] // tm
    assert nc * tm == yb_c.shape[0] and nl * tm == yb_l.shape[0] and nc + nl == m // tm
    y_spec = pl.BlockSpec((tm, kw), lambda i, j: (i, 0))
    ctx_spec = pl.BlockSpec((tm, kw), lambda i, j: (jnp.minimum(i, nc - 1), 0))
    lat_spec = pl.BlockSpec((tm, kw), lambda i, j: (jnp.maximum(i - nc, 0), 0))
    w_spec = pl.BlockSpec((None, kw, tn), lambda i, j: (layer, 0, j))

    def gate_spec(col):
        return pl.BlockSpec((tm, tn), lambda i, j: (i, col // tn + j))

    return pl.pallas_call(
        functools.partial(_merge_kernel, n_ctx_tiles=nc),
        grid=(m // tm, d // tn),
        in_specs=[y_spec, ctx_spec, lat_spec, ctx_spec, lat_spec, w_spec, w_spec, w_spec,
                  gate_spec(COL_GA), gate_spec(COL_GB), gate_spec(COL_GC)],
        out_specs=pl.BlockSpec((tm, tn), lambda i, j: (i, j)),
        out_shape=jax.ShapeDtypeStruct((m, d), BF16),
        compiler_params=_cparams(2, 48 * 2**20),
        name="merge",
    )(ya, yb_c, yb_l, yc_c, yc_l, wa, wb, wc, z, z, z)


def _extract_top(x, count, out_ref, with_rank=False):
    rank = jnp.full(x.shape, float(PEER_TOPK), F32)
    for j in range(count):
        m = jnp.max(x, axis=0, keepdims=True)
        out_ref[j:j + 1, :] = m
        hit = x >= m
        if with_rank and j < PEER_TOPK:
            rank = jnp.where(hit, float(j), rank)
        if j + 1 < count:
            x = jnp.where(hit, NEG, x)
    return rank if with_rank else None


def _route_kernel(q_ref, k1_ref, k2_ref, cnt_ref, r2_ref, e2_ref, c1_ref, t1, t2, c):
    nk = PEER_NKEYS
    k = PEER_TOPK

    def scores(keys, qt):
        kh, kl = _split_bf16(keys)
        qh, ql = _split_bf16(qt)
        return _dot(kh, qh) + (_dot(kh, ql) + _dot(kl, qh))

    s1 = scores(k1_ref[...], q_ref[0:nk, :])
    s2 = scores(k2_ref[...], q_ref[nk:2 * nk, :])
    _extract_top(s1, k + 1, t1)
    rank2 = _extract_top(s2, k + 1, t2, with_rank=True)
    n = s1.shape[1]
    rid8 = lax.broadcasted_iota(jnp.int32, (8, n), 0)
    extra = jnp.where(rid8 == 0, t1[k:k + 1, :] + t2[0:1, :],
                      jnp.where(rid8 == 1, t1[0:1, :] + t2[k:k + 1, :], NEG))
    cand = jnp.concatenate(
        [t1[0:1, :] + t2[0:k, :]] + [t1[a:a + 1, :] + t2[0:8, :] for a in range(1, k)] + [extra], axis=0)
    _extract_top(cand, k + 1, c)
    tau = 0.5 * (c[k - 1:k, :] + c[k:k + 1, :])
    zsum = jnp.sum(jnp.exp(c[0:k, :] - c[0:1, :]), axis=0, keepdims=True)
    th = tau - s1
    cnt = jnp.zeros_like(th)
    for b in range(k):
        cnt = jnp.where(th <= t2[b:b + 1, :], float(b + 1), cnt)
    cnt_ref[...] = cnt
    r2_ref[...] = rank2.astype(BF16)
    e2_ref[...] = jnp.exp(s2 - t2[0:1, :]).astype(BF16)
    c1_ref[...] = jnp.exp(s1 - t1[0:1, :]) / zsum


def _peer_route(qt, k1, k2, nt=512):
    m = qt.shape[1]
    nk = PEER_NKEYS
    out_f32 = jax.ShapeDtypeStruct((PEER_HEADS, nk, m), F32)
    out_b16 = jax.ShapeDtypeStruct((PEER_HEADS, nk, m), BF16)
    ospec = pl.BlockSpec((None, nk, nt), lambda i, h: (h, 0, i))
    kspec = pl.BlockSpec((nk, nk), lambda i, h: (0, 0))
    return pl.pallas_call(
        _route_kernel,
        grid=(m // nt, PEER_HEADS),
        in_specs=[pl.BlockSpec((2 * nk, nt), lambda i, h: (h, i)), kspec, kspec],
        out_specs=[ospec] * 4,
        out_shape=[out_f32, out_b16, out_b16, out_f32],
        scratch_shapes=[pltpu.VMEM((24, nt), F32)] * 3,
        compiler_params=_cparams(2, 24 * 2**20),
        name="peer_route",
    )(qt, k1, k2)


PEER_TE = 1024
PEER_RBLK = 4


def _peer_kernel(ht_ref, cnt_ref, r2_in_ref, e2_in_ref, c1_ref, u_ref, vt_ref, o_ref,
                 act_ref, g_ref, r2_ref, e2_ref, *, te, nt):
    j = pl.program_id(1)
    nk = PEER_NKEYS
    groups = te // nk

    @pl.when(j == 0)
    def _():
        o_ref[...] = jnp.zeros_like(o_ref)
        r2_ref[...] = r2_in_ref[...]
        e2_ref[...] = e2_in_ref[...]

    act_ref[...] = _dot(u_ref[...], ht_ref[...]).astype(BF16)

    def row_bf16(ref, h, r, lanes):
        row = jnp.broadcast_to(ref[h, r:r + 1, lanes], (16, 128)).astype(BF16)
        return pltpu.repeat(row, nk // 16, axis=0)

    for lb in range(nt // 128):
        lanes = slice(lb * 128, (lb + 1) * 128)
        for rb in range(groups // PEER_RBLK):
            ws = [jnp.zeros((nk, 128), BF16) for _ in range(PEER_RBLK)]
            for h in range(PEER_HEADS):
                r2t = r2_ref[h * nk:(h + 1) * nk, lanes]
                e2t = e2_ref[h * nk:(h + 1) * nk, lanes]
                for rr in range(PEER_RBLK):
                    r = rb * PEER_RBLK + rr
                    hit = r2t < row_bf16(cnt_ref, h, r, lanes)
                    ws[rr] = ws[rr] + jnp.where(hit, e2t * row_bf16(c1_ref, h, r, lanes), 0.0)
            for rr in range(PEER_RBLK):
                rows = slice((rb * PEER_RBLK + rr) * nk, (rb * PEER_RBLK + rr + 1) * nk)
                g_ref[rows, lanes] = jax.nn.gelu(act_ref[rows, lanes]) * ws[rr]

    o_ref[...] += _dot(vt_ref[...], g_ref[...])


def _peer(ht, cnt, r2, e2, c1, u, vt, layer, nt=512):
    d, m = ht.shape
    te = PEER_TE
    nk = PEER_NKEYS
    groups = te // nk
    assert groups == 8, "first-key rows of a tile must fill one sublane group"
    rspec = pl.BlockSpec((PEER_HEADS * nk, nt), lambda i, j: (0, i))
    gspec = pl.BlockSpec((PEER_HEADS, groups, nt), lambda i, j: (0, j, i))
    est = (2 * (d * nt * 2 + 2 * PEER_HEADS * nk * nt * 2 + 2 * te * d * 2 + d * nt * 4)
           + 2 * te * nt * 2 + te * nt * 4 + d * nt * 4)
    return pl.pallas_call(
        functools.partial(_peer_kernel, te=te, nt=nt),
        grid=(m // nt, N_EXPERTS // te),
        in_specs=[
            pl.BlockSpec((d, nt), lambda i, j: (0, i)),
            gspec, rspec, rspec, gspec,
            pl.BlockSpec((None, te, d), lambda i, j: (layer, j, 0)),
            pl.BlockSpec((None, None, d, te), lambda i, j: (layer, j, 0, 0)),
        ],
        out_specs=pl.BlockSpec((d, nt), lambda i, j: (0, i)),
        out_shape=jax.ShapeDtypeStruct((d, m), F32),
        scratch_shapes=[pltpu.VMEM((te, nt), BF16), pltpu.VMEM((te, nt), BF16),
                        pltpu.VMEM((PEER_HEADS * nk, nt), BF16), pltpu.VMEM((PEER_HEADS * nk, nt), BF16)],
        compiler_params=_cparams(2, est + 2 * PEER_HEADS * nk * nt * 2 + 6 * 2**20),
        name="peer_experts",
    )(ht, cnt, r2.reshape(PEER_HEADS * nk, m), e2.reshape(PEER_HEADS * nk, m), c1, u, vt)


def kernel(x_prompt, x_sample, cache_k, cache_v, state_gla, c, c_ctx, w_ada, b_ada, w_in, a_ln_g, a_ln_b, a_ws, a_bs, na_rpb, gla_wg_f, gla_bg_f, gla_wg_b, gla_bg_b, gla_norm_g, w_br_a, w_br_b, w_br_c, w_out, ln1_g, ln1_b, ln2_g, ln2_b, peer_wq, peer_k1, peer_k2, peer_u, peer_v):
    n_ctx, n_lat = x_prompt.shape[0], x_sample.shape[0]
    n_layers = w_in.shape[0]
    d = D_MODEL
    mc, ml = n_ctx * SEQ, n_lat * DEC_SEQ
    assert mc % DEC_SEQ == 0, "latent row blocks must start on a DEC_SEQ boundary of the shared token axis"
    assert n_lat + 1 <= 16

    cc = jnp.zeros((16, d), F32).at[0].set(c_ctx).at[1:1 + n_lat].set(c)
    mod = _ada(cc, w_ada, b_ada).reshape(n_layers, 16, N_MOD, 1, d)

    def mod_vec(layer, which):
        return mod[layer, :, which]

    w_main = jnp.concatenate([w_in[:, :, :LR_START], w_in[:, :, LR_END:]], axis=-1).astype(BF16)
    w_lr = jnp.pad(w_in[:, :, LR_START:LR_END], ((0, 0), (0, 0), (0, 128 - (LR_END - LR_START)))).astype(BF16)
    wgf = jnp.pad(gla_wg_f, ((0, 0), (0, 128 - GLA_RANK), (0, 0)))
    wgb = jnp.pad(gla_wg_b, ((0, 0), (GLA_RANK, 128 - 2 * GLA_RANK), (0, 0)))
    wbr_a, wbr_b, wbr_c = w_br_a.astype(BF16), w_br_b.astype(BF16), w_br_c.astype(BF16)
    w_o = w_out.astype(BF16)
    wq_t = jnp.swapaxes(peer_wq, 1, 2).astype(BF16)
    u_b = peer_u.astype(BF16)
    v_t = jnp.swapaxes(peer_v.reshape(n_layers, N_EXPERTS // PEER_TE, PEER_TE, d), 2, 3).astype(BF16)
    rope_tabs = _rope_tables(DEC_SEQ)
    bias_tab = _na_bias_tables(na_rpb)

    x = jnp.concatenate([x_prompt.reshape(mc, d), x_sample.reshape(ml, d)], axis=0)
    h = _modulate(x, mod_vec(0, 0), mod_vec(0, 1), mc)

    new_k, new_v, new_s = [], [], []
    for l in range(n_layers):
        z = _matmul(h, w_main, BF16, 1024, 2048, "in_proj", b_resident=True, layer=l)
        lr = _matmul(h, w_lr[l], F32, 2048, 128, "lr_proj")
        ya = _mixer_a(z, a_ln_g[l], a_ln_b[l], a_ws[l], a_bs[l])
        yb_c, k_l, v_l = _attn_ctx(z, n_ctx)
        yb_l = _attn_lat(z, cache_k, cache_v, bias_tab, l, n_lat, mc // DEC_SEQ)
        gla_args = (wgf[l], wgb[l], gla_bg_f[l], gla_bg_b[l], gla_norm_g[l])
        yc_c, s_l = _gla(z, lr, *gla_args, n_ctx, SEQ, 0, "gla_ctx", 4)
        yc_l, _ = _gla(z, lr, *gla_args, n_lat, DEC_SEQ, mc // DEC_SEQ, "gla_lat", 2,
                       rope_tabs=rope_tabs, s0=state_gla, layer=l)
        new_k.append(k_l)
        new_v.append(v_l)
        new_s.append(s_l)
        mrg = _merge(ya, yb_c, yb_l, yc_c, yc_l, wbr_a, wbr_b, wbr_c, z, l)
        x, h2t = _res_ln(x, mrg, mod_vec(l, 2), ln1_g[l], ln1_b[l], mod_vec(l, 3), mod_vec(l, 4), mc, "out_ln1",
                         h_transposed=True, w=w_o, layer=l, tm=512)
        qt = _matmul(wq_t, h2t, F32, 512, 1024, "peer_q", layer=l)
        cnt, r2, e2, c1 = _peer_route(qt, peer_k1[l], peer_k2[l])
        pt = _peer(h2t, cnt, r2, e2, c1, u_b, v_t, l)
        ln2_args = (x, pt, mod_vec(l, 5), ln2_g[l], ln2_b[l])
        if l + 1 < n_layers:
            x, h = _res_ln(*ln2_args, mod_vec(l + 1, 0), mod_vec(l + 1, 1), mc, "res_ln2", y_transposed=True)
        else:
            x_ctx, _ = _res_ln(*ln2_args, mod_vec(l, 0), mod_vec(l, 1), mc, "res_ln2_ctx", y_transposed=True,
                               rows=(0, mc))
            x_lat, _ = _res_ln(*ln2_args, mod_vec(l, 0), mod_vec(l, 1), mc, "res_ln2_lat", y_transposed=True,
                               rows=(mc, ml))

    y_prompt = x_ctx.reshape(n_ctx, SEQ, d)
    y_sample = x_lat.reshape(n_lat, DEC_SEQ, d)
    return (y_prompt, y_sample, jnp.stack(new_k, axis=1), jnp.stack(new_v, axis=1), jnp.stack(new_s, axis=1))
```
